```python
import math
import jax, jax.numpy as jnp
from jax import lax
import numpy as np

D_MODEL = 1024
BATCH = 8
SEQ = 2048
DEPTH = 2

RWKV_HEADS = 8
RWKV_HD = 64
RWKV_W = RWKV_HEADS * RWKV_HD
DECAY_LORA = 64
ICLR_LORA = 64
VRES_LORA = 32
GATE_LORA = 128
LNX_EPS = 64e-5
SB_HEADS = 8
SB_HD = 64
SB_W = SB_HEADS * SB_HD
SB_BLOCK = 128
MEM_LEN = 256
MEM_HEADS = 4
MEM_HD = 128
MEM_W = MEM_HEADS * MEM_HD
N_BRANCH = 3
BRANCH_W = 512
D_FF = 2816
CONV_W = 3
NORM_EPS = 1e-6

RWKV_SPLITS = (RWKV_W, RWKV_W, RWKV_W, DECAY_LORA, ICLR_LORA, GATE_LORA)
RWKV_COLS = sum(RWKV_SPLITS)
IN_COLS = RWKV_COLS + 3 * SB_W + MEM_W + N_BRANCH * D_MODEL

kernel_name = "hybrid_rwkv7_stickbreak_memxattn_convffn"


def rms_norm(x, g, eps=NORM_EPS):
    xf = x.astype(jnp.float32)
    y = xf * lax.rsqrt(jnp.mean(xf * xf, axis=-1, keepdims=True) + eps)
    return (y * g.astype(jnp.float32)).astype(x.dtype)


def split_heads(t, h, d):
    return t.reshape(t.shape[:-1] + (h, d))


def token_shift(p, mu):
    prev = jnp.pad(p, ((0, 0), (1, 0), (0, 0)))[:, :-1]
    return p + (prev - p) * mu


def rwkv7_recurrence(r, decay, k, v, kk, a):
    b, t, h, n = r.shape
    xs = tuple(jnp.moveaxis(z, 1, 0) for z in (r, decay, k, v, kk, a))

    def step(S, inp):
        r_t, w_t, k_t, v_t, kk_t, a_t = inp
        sa = jnp.einsum('bhij,bhj->bhi', S, -kk_t)
        S = (S * w_t[:, :, None, :]
             + sa[..., None] * (kk_t * a_t)[:, :, None, :]
             + v_t[..., None] * k_t[:, :, None, :])
        y_t = jnp.einsum('bhij,bhj->bhi', S, r_t)
        return S, y_t

    S0 = jnp.zeros((b, h, n, n), jnp.float32)
    _, ys = lax.scan(step, S0, xs)
    return jnp.moveaxis(ys, 0, 1)


def rwkv7_branch(p, mu, w0, w2, a0, a2, g2, k_k, k_a, r_k, lnx_g, lnx_b, v_first, vres):
    dt = p.dtype
    p = token_shift(p, mu)
    idx = list(np.cumsum(RWKV_SPLITS)[:-1])
    r, k, v, wl, al, gl = jnp.split(p, idx, axis=-1)
    f32 = jnp.float32
    w = -jax.nn.softplus(-(w0 + jnp.tanh(wl) @ w2).astype(f32)) - 0.5
    decay = jnp.exp(-jnp.exp(w))
    a = jax.nn.sigmoid((a0 + al @ a2).astype(f32))
    g = jax.nn.sigmoid(gl) @ g2
    if vres is None:
        v_first = v
    else:
        v0, v1, v2 = vres
        v = v + (v_first - v) * jax.nn.sigmoid(v0 + (v @ v1) @ v2)
    kk = split_heads((k * k_k).astype(f32), RWKV_HEADS, RWKV_HD)
    kk = kk / jnp.maximum(jnp.sqrt(jnp.sum(kk * kk, axis=-1, keepdims=True)), 1e-12)
    k = k.astype(f32) * (1.0 + (a - 1.0) * k_a.astype(f32))
    rh = split_heads(r.astype(f32), RWKV_HEADS, RWKV_HD)
    kh = split_heads(k, RWKV_HEADS, RWKV_HD)
    vh = split_heads(v.astype(f32), RWKV_HEADS, RWKV_HD)
    ah = split_heads(a, RWKV_HEADS, RWKV_HD)
    dh = split_heads(decay, RWKV_HEADS, RWKV_HD)
    y = rwkv7_recurrence(rh, dh, kh, vh, kk, ah)
    mean = jnp.mean(y, axis=-1, keepdims=True)
    var = jnp.mean(jnp.square(y - mean), axis=-1, keepdims=True)
    y = (y - mean) * lax.rsqrt(var + LNX_EPS)
    y = y.reshape(y.shape[:2] + (RWKV_W,)) * lnx_g.astype(f32) + lnx_b.astype(f32)
    bonus = jnp.sum(rh * kh * r_k.astype(f32), axis=-1, keepdims=True) * vh
    y = (y + bonus.reshape(y.shape)) * g.astype(f32)
    return y.astype(dt), v_first


def stick_breaking_attention(q, k, v):
    t_len = q.shape[2]
    scale = SB_HD ** -0.5
    outs = []
    for blk in range(t_len // SB_BLOCK):
        start = blk * SB_BLOCK
        end = start + SB_BLOCK
        qb = q[:, :, start:end]
        kb = k[:, :, :end]
        vb = v[:, :, :end]
        z = jnp.einsum('bhqd,bhkd->bhqk', qb, kb).astype(jnp.float32) * scale
        t_idx = start + jnp.arange(SB_BLOCK)
        s_idx = jnp.arange(end)
        mask = s_idx[None, :] < t_idx[:, None]
        log_1mb = jnp.where(mask, jax.nn.log_sigmoid(-z), 0.0)
        log_between = lax.cumsum(log_1mb, axis=3, reverse=True) - log_1mb
        attn = jnp.where(mask, jnp.exp(jax.nn.log_sigmoid(z) + log_between), 0.0)
        outs.append(jnp.einsum('bhqk,bhkd->bhqd', attn.astype(v.dtype), vb))
    return jnp.concatenate(outs, axis=2)


def stick_breaking_branch(p, q_g, k_g):
    b, t, _ = p.shape
    q, k, v = jnp.split(p, 3, axis=-1)
    q = rms_norm(split_heads(q, SB_HEADS, SB_HD), q_g).transpose(0, 2, 1, 3)
    k = rms_norm(split_heads(k, SB_HEADS, SB_HD), k_g).transpose(0, 2, 1, 3)
    v = split_heads(v, SB_HEADS, SB_HD).transpose(0, 2, 1, 3)
    o = stick_breaking_attention(q, k, v)
    return o.transpose(0, 2, 1, 3).reshape(b, t, SB_W)


def memory_branch(q, mem, mem_g, w_kv, q_g, k_g):
    b, t, _ = q.shape
    kv = rms_norm(mem, mem_g) @ w_kv
    mk, mv = jnp.split(kv, 2, axis=-1)
    qh = rms_norm(split_heads(q, MEM_HEADS, MEM_HD), q_g)
    kh = rms_norm(split_heads(mk, MEM_HEADS, MEM_HD), k_g)
    vh = split_heads(mv, MEM_HEADS, MEM_HD)
    s = jnp.einsum('bthd,bmhd->bhtm', qh, kh).astype(jnp.float32) * (MEM_HD ** -0.5)
    pr = jax.nn.softmax(s, axis=-1).astype(vh.dtype)
    o = jnp.einsum('bhtm,bmhd->bthd', pr, vh)
    return o.reshape(b, t, MEM_W)


def conv_ffn(x, g, w_up, conv_w, conv_b, w_down):
    h = rms_norm(x, g) @ w_up
    t = h.shape[1]
    hp = jnp.pad(h, ((0, 0), (CONV_W - 1, 0), (0, 0)))
    acc = conv_b + hp[:, 0:t] * conv_w[0]
    for i in range(1, CONV_W):
        acc = acc + hp[:, i:i + t] * conv_w[i]
    gate, val = jnp.split(acc, 2, axis=-1)
    return (jax.nn.silu(gate) * val) @ w_down


def setup_inputs(seed: int = 0) -> dict:
    key = jax.random.key(seed)
    ks = jax.random.split(key, 40)
    f32 = jnp.float32
    nrm = lambda k, s, sc: jax.random.normal(k, s, f32) * sc
    gain = lambda k, s: 1.0 + 0.05 * jax.random.normal(k, s, f32)
    L = DEPTH
    return {
        "x": nrm(ks[0], (BATCH, SEQ, D_MODEL), 1.0),
        "mem": nrm(ks[1], (BATCH, MEM_LEN, D_MODEL), 1.0),
        "norm1_g": gain(ks[2], (L, D_MODEL)),
        "w_in": nrm(ks[3], (L, D_MODEL, IN_COLS), D_MODEL ** -0.5),
        "shift_mu": jax.random.uniform(ks[4], (L, RWKV_COLS), f32, 0.0, 1.0),
        "decay_w0": jax.random.uniform(ks[5], (L, RWKV_W), f32, -6.0, 1.0),
        "decay_w2": nrm(ks[6], (L, DECAY_LORA, RWKV_W), 0.5 * DECAY_LORA ** -0.5),
        "iclr_a0": nrm(ks[7], (L, RWKV_W), 0.1),
        "iclr_a2": nrm(ks[8], (L, ICLR_LORA, RWKV_W), ICLR_LORA ** -0.5),
        "gate_g2": nrm(ks[9], (L, GATE_LORA, RWKV_W), GATE_LORA ** -0.5),
        "k_k": 0.85 + 0.05 * jax.random.normal(ks[10], (L, RWKV_W), f32),
        "k_a": gain(ks[11], (L, RWKV_W)),
        "r_k": nrm(ks[12], (L, RWKV_HEADS, RWKV_HD), 0.1),
        "lnx_g": gain(ks[13], (L, RWKV_W)),
        "lnx_b": nrm(ks[14], (L, RWKV_W), 0.02),
        "vres_v0": 1.0 + 0.1 * jax.random.normal(ks[15], (L - 1, RWKV_W), f32),
        "vres_v1": nrm(ks[16], (L - 1, RWKV_W, VRES_LORA), RWKV_W ** -0.5),
        "vres_v2": nrm(ks[17], (L - 1, VRES_LORA, RWKV_W), VRES_LORA ** -0.5),
        "sb_q_norm_g": gain(ks[18], (L, SB_HD)),
        "sb_k_norm_g": gain(ks[19], (L, SB_HD)),
        "mem_norm_g": gain(ks[20], (L, D_MODEL)),
        "w_mem_kv": nrm(ks[21], (L, D_MODEL, 2 * MEM_W), D_MODEL ** -0.5),
        "mem_q_norm_g": gain(ks[22], (L, MEM_HD)),
        "mem_k_norm_g": gain(ks[23], (L, MEM_HD)),
        "w_branch": nrm(ks[24], (L, N_BRANCH, BRANCH_W, D_MODEL), BRANCH_W ** -0.5),
        "w_out": nrm(ks[25], (L, D_MODEL, D_MODEL), D_MODEL ** -0.5),
        "norm2_g": gain(ks[26], (L, D_MODEL)),
        "w_up": nrm(ks[27], (L, D_MODEL, 2 * D_FF), D_MODEL ** -0.5),
        "conv_w": nrm(ks[28], (L, CONV_W, 2 * D_FF), CONV_W ** -0.5),
        "conv_b": nrm(ks[29], (L, 2 * D_FF), 0.02),
        "w_down": nrm(ks[30], (L, D_FF, D_MODEL), D_FF ** -0.5),
    }


def reference(x, mem, norm1_g, w_in, shift_mu, decay_w0, decay_w2, iclr_a0, iclr_a2, gate_g2,
              k_k, k_a, r_k, lnx_g, lnx_b, vres_v0, vres_v1, vres_v2, sb_q_norm_g, sb_k_norm_g,
              mem_norm_g, w_mem_kv, mem_q_norm_g, mem_k_norm_g, w_branch, w_out, norm2_g,
              w_up, conv_w, conv_b, w_down):
    b, t, _ = x.shape
    cut = [RWKV_COLS, RWKV_COLS + 3 * SB_W, RWKV_COLS + 3 * SB_W + MEM_W]
    v_first = None
    for l in range(DEPTH):
        h = rms_norm(x, norm1_g[l])
        proj = h @ w_in[l]
        p_rwkv, p_sb, q_mem, gate_logits = jnp.split(proj, cut, axis=-1)
        vres = None if l == 0 else (vres_v0[l - 1], vres_v1[l - 1], vres_v2[l - 1])
        y_a, v_first = rwkv7_branch(p_rwkv, shift_mu[l], decay_w0[l], decay_w2[l], iclr_a0[l],
                                    iclr_a2[l], gate_g2[l], k_k[l], k_a[l], r_k[l], lnx_g[l],
                                    lnx_b[l], v_first, vres)
        y_b = stick_breaking_branch(p_sb, sb_q_norm_g[l], sb_k_norm_g[l])
        y_m = memory_branch(q_mem, mem, mem_norm_g[l], w_mem_kv[l], mem_q_norm_g[l], mem_k_norm_g[l])
        ys = jnp.stack([y_a, y_b.astype(y_a.dtype), y_m.astype(y_a.dtype)], axis=2)
        branch_out = jnp.einsum('btnc,ncd->btnd', ys, w_branch[l])
        gates = jax.nn.sigmoid(gate_logits.reshape(b, t, N_BRANCH, D_MODEL))
        merged = jnp.sum(gates * branch_out, axis=2)
        x = x + merged @ w_out[l]
        x = x + conv_ffn(x, norm2_g[l], w_up[l], conv_w[l], conv_b[l], w_down[l])
    return x
```

```python
import functools

import jax
import jax.numpy as jnp
from jax import lax
from jax.experimental import pallas as pl
from jax.experimental.pallas import tpu as pltpu

F32 = jnp.float32
BF16 = jnp.bfloat16

D_MODEL = 1024
N_HEADS = 8
HEAD_DIM = 64
BRANCH_W = 512
RWKV_COLS = 1792
RWKV_PAD = 2048
P_COLS = 7168
MEM_LEN = 256
MEM_HEADS = 4
MEM_HD = 128
D_FF = 2816
NORM_EPS = 1e-6
LNX_EPS = 64e-5
LANES = 128
VMEM_LIMIT = 56 * 1024 * 1024

NT_DIMS = (((1,), (1,)), ((), ()))


def _rms(xf, g, eps=NORM_EPS):
    return xf * lax.rsqrt(jnp.mean(xf * xf, axis=-1, keepdims=True) + eps) * g


def _sigmoid(x):
    return 1.0 / (1.0 + jnp.exp(-x))


def _softplus(x):
    return jnp.maximum(x, 0.0) + jnp.log(1.0 + jnp.exp(-jnp.abs(x)))


def _dot(a, b):
    return jnp.dot(a, b, preferred_element_type=F32)


def _split3(x):
    h1 = x.astype(BF16)
    r1 = x - h1.astype(F32)
    h2 = r1.astype(BF16)
    h3 = (r1 - h2.astype(F32)).astype(BF16)
    return h1, h2, h3


def _dot_ones(x, ones_bf16):
    h1, h2, h3 = _split3(x)
    return _dot(h1, ones_bf16) + _dot(h2, ones_bf16) + _dot(h3, ones_bf16)


def _params(*sem):
    return pltpu.CompilerParams(dimension_semantics=sem, vmem_limit_bytes=VMEM_LIMIT)


def _proj_body(x_ref, g_ref, w_ref, o_ref, h_scr):
    @pl.when(pl.program_id(1) == 0)
    def _():
        h_scr[...] = _rms(x_ref[...], g_ref[...]).astype(BF16)

    o_ref[...] = _dot(h_scr[...], w_ref[...])


def _proj(x2d, g, w_bf16, tq=1024, tc=1024):
    n = x2d.shape[0]
    return pl.pallas_call(
        _proj_body,
        grid=(n // tq, P_COLS // tc),
        in_specs=[
            pl.BlockSpec((tq, D_MODEL), lambda i, j: (i, 0)),
            pl.BlockSpec((1, D_MODEL), lambda i, j: (0, 0)),
            pl.BlockSpec((D_MODEL, tc), lambda i, j: (0, j)),
        ],
        out_specs=pl.BlockSpec((tq, tc), lambda i, j: (i, j)),
        out_shape=jax.ShapeDtypeStruct((n, P_COLS), F32),
        scratch_shapes=[pltpu.VMEM((tq, D_MODEL), BF16)],
        compiler_params=_params("parallel", "arbitrary"),
        name="proj",
    )(x2d, g, w_bf16)


def _prep_body(has_vres, tq, *refs):
    if has_vres:
        (p_ref, mu_ref, w0_ref, w2_ref, a0_ref, a2_ref, g2_ref, kk_ref, ka_ref, rk_ref, bd_ref,
         vf_ref, v0_ref, v1_ref, v2_ref,
         r_o, dec_o, k_o, v_o, nkk_o, bb_o, g_o, bonus_o, carry) = refs
    else:
        (p_ref, mu_ref, w0_ref, w2_ref, a0_ref, a2_ref, g2_ref, kk_ref, ka_ref, rk_ref, bd_ref,
         r_o, dec_o, k_o, v_o, nkk_o, bb_o, g_o, bonus_o, carry) = refs

    @pl.when(pl.program_id(1) == 0)
    def _():
        carry[...] = jnp.zeros_like(carry)

    p = p_ref[0][:, :RWKV_COLS]
    row = lax.broadcasted_iota(jnp.int32, (tq, 1), 0)
    prev = jnp.where(row == 0, carry[7:8, :], pltpu.roll(p, 1, 0))
    carry[...] = p[tq - 8:, :]
    ps = p + (prev - p) * mu_ref[...]

    r = ps[:, 0:512]
    k = ps[:, 512:1024]
    v = ps[:, 1024:1536]
    wa = ps[:, 1536:1664]
    gl = ps[:, 1664:1792]

    wlog = -_softplus(-(w0_ref[...] + _dot(jnp.tanh(wa).astype(BF16), w2_ref[...]))) - 0.5
    dec = jnp.exp(-jnp.exp(wlog))
    a = _sigmoid(a0_ref[...] + _dot(wa.astype(BF16), a2_ref[...]))
    g = _dot(_sigmoid(gl).astype(BF16), g2_ref[...])
    if has_vres:
        lo = _dot(v.astype(BF16), v1_ref[...])
        v = v + (vf_ref[0] - v) * _sigmoid(v0_ref[...] + _dot(lo.astype(BF16), v2_ref[...]))
    bd = bd_ref[...]
    kk = k * kk_ref[...]
    kk = kk / jnp.maximum(jnp.sqrt(_dot_ones(kk * kk, bd)), 1e-12)
    k2 = k * (1.0 + (a - 1.0) * ka_ref[...])
    bonus = _dot_ones(r * k2 * rk_ref[...], bd) * v

    r_o[0] = r
    dec_o[0] = dec
    k_o[0] = k2
    v_o[0] = v
    nkk_o[0] = -kk
    bb_o[0] = kk * a
    g_o[0] = g
    bonus_o[0] = bonus


def _prep(P, vecs, mats, bd, vres, tq=256):
    b, t, _ = P.shape
    has_vres = vres is not None
    tok = pl.BlockSpec((1, tq, BRANCH_W), lambda i, j: (i, j, 0))

    def full(a):
        return pl.BlockSpec(a.shape, lambda i, j: (0,) * a.ndim)

    mu, w0, a0, kk, ka, rk = vecs
    w2, a2, g2 = mats
    args = [P, mu, w0, w2, a0, a2, g2, kk, ka, rk, bd]
    in_specs = [pl.BlockSpec((1, tq, RWKV_PAD), lambda i, j: (i, j, 0))] + [full(a) for a in args[1:]]
    if has_vres:
        vf, v0, v1, v2 = vres
        args += [vf, v0, v1, v2]
        in_specs += [tok, full(v0), full(v1), full(v2)]
    out = jax.ShapeDtypeStruct((b, t, BRANCH_W), F32)
    return pl.pallas_call(
        functools.partial(_prep_body, has_vres, tq),
        grid=(b, t // tq),
        in_specs=in_specs,
        out_specs=[tok] * 8,
        out_shape=[out] * 8,
        scratch_shapes=[pltpu.VMEM((8, RWKV_COLS), F32)],
        compiler_params=_params("parallel", "arbitrary"),
        name="rwkv_prep",
    )(*args)


def _rec_body(tc, w_ref, nk_ref, b_ref, k_ref, r_ref, v_ref, y_ref, s_scr):
    @pl.when(pl.program_id(0) == 0)
    def _():
        s_scr[...] = jnp.zeros_like(s_scr)

    def step(t, carry):
        w = w_ref[t]
        nk = nk_ref[t]
        bb = b_ref[t]
        kk = k_ref[t]
        rr = r_ref[t]
        for i in range(HEAD_DIM // 2):
            s = s_scr[i]
            sa = jnp.sum(s * nk, axis=0, keepdims=True)
            vi = v_ref[t, pl.ds(i, 1), :]
            sn = s * w + (sa * bb + vi * kk)
            s_scr[i] = sn
            y_ref[t, pl.ds(i, 1), :] = jnp.sum(sn * rr, axis=0, keepdims=True)
        return carry

    lax.fori_loop(0, tc, step, 0)


def _recurrence(dec, nkk, bb, k2, r, v, tc=32):
    t = dec.shape[0]
    jspec = pl.BlockSpec((tc, HEAD_DIM, LANES), lambda i: (i, 0, 0))
    ispec = pl.BlockSpec((tc, HEAD_DIM // 2, LANES), lambda i: (i, 0, 0))
    return pl.pallas_call(
        functools.partial(_rec_body, tc),
        grid=(t // tc,),
        in_specs=[jspec] * 5 + [ispec],
        out_specs=ispec,
        out_shape=jax.ShapeDtypeStruct((t, HEAD_DIM // 2, LANES), F32),
        scratch_shapes=[pltpu.VMEM((HEAD_DIM // 2, HEAD_DIM, LANES), F32)],
        compiler_params=_params("arbitrary"),
        name="rwkv_rec",
    )(dec, nkk, bb, k2, r, v)


def _to_key_lanes(a):
    b, t, _ = a.shape
    a = a.reshape(b, t, N_HEADS, HEAD_DIM).transpose(1, 3, 0, 2).reshape(t, HEAD_DIM, b * N_HEADS)
    return jnp.concatenate([a, a], axis=-1)


def _to_val_lanes(a):
    b, t, _ = a.shape
    a = a.reshape(b, t, N_HEADS, 2, HEAD_DIM // 2).transpose(1, 4, 3, 0, 2)
    return a.reshape(t, HEAD_DIM // 2, 2 * b * N_HEADS)


def _from_val_lanes(y, b):
    t = y.shape[0]
    y = y.reshape(t, HEAD_DIM // 2, 2, b, N_HEADS).transpose(3, 0, 4, 2, 1)
    return y.reshape(b, t, BRANCH_W)


def _sb_body(tq, q_ref, k_ref, v_ref, qg_ref, kg_ref, u_ref, o_ref, kn_scr, vb_scr):
    qi = pl.program_id(2)
    lane = lax.broadcasted_iota(jnp.int32, (1, LANES), 1)
    first = lane < HEAD_DIM
    m_a = first.astype(F32)
    m_b = 1.0 - m_a

    def headnorm(x, g):
        x2 = x * x
        s_a = jnp.sum(x2 * m_a, axis=-1, keepdims=True)
        s_b = jnp.sum(x2 * m_b, axis=-1, keepdims=True)
        ms = jnp.where(first, s_a, s_b) * (1.0 / HEAD_DIM)
        return x * lax.rsqrt(ms + NORM_EPS) * g

    @pl.when(qi == 0)
    def _():
        kn_scr[...] = headnorm(k_ref[0], kg_ref[...]).astype(BF16)
        vb_scr[...] = v_ref[0].astype(BF16)

    qn = headnorm(q_ref[0], qg_ref[...]) * (HEAD_DIM ** -0.5)
    q_heads = ((qn * m_a).astype(BF16), (qn * m_b).astype(BF16))
    u = u_ref[...]
    causal = (lax.broadcasted_iota(jnp.int32, (tq, tq), 1)
              < lax.broadcasted_iota(jnp.int32, (tq, tq), 0))

    def block(qh, kb, vb, run, acc, masked):
        z = lax.dot_general(qh, kb, NT_DIMS, preferred_element_type=F32)
        l1m = -_softplus(z)
        if masked:
            l1m = jnp.where(causal, l1m, 0.0)
        hi = l1m.astype(BF16)
        lo = (l1m - hi.astype(F32)).astype(BF16)
        ce = _dot(hi, u) + _dot(lo, u)
        att = jnp.exp(z + l1m + run + ce[:, :tq])
        if masked:
            att = jnp.where(causal, att, 0.0)
        return run + ce[:, tq:], acc + _dot(att.astype(BF16), vb)

    def kv_at(kj):
        ks = pl.multiple_of(kj * tq, tq)
        return kn_scr[pl.ds(ks, tq), :], vb_scr[pl.ds(ks, tq), :]

    zero = jnp.zeros((tq, LANES), F32)
    kb, vb = kv_at(qi)
    carry = []
    for qh in q_heads:
        carry += list(block(qh, kb, vb, zero, zero, True))

    def body(n, c):
        kb, vb = kv_at(qi - 1 - n)
        out = []
        for h, qh in enumerate(q_heads):
            out += list(block(qh, kb, vb, c[2 * h], c[2 * h + 1], False))
        return tuple(out)

    c = lax.fori_loop(0, qi, body, tuple(carry))
    o_ref[0] = jnp.where(first, c[1], c[3])


def _sb_attention(P, qg, kg, u, tq=128):
    b, t, _ = P.shape
    npair = N_HEADS // 2
    qoff, koff, voff = (RWKV_PAD // LANES, (RWKV_PAD + 512) // LANES, (RWKV_PAD + 1024) // LANES)
    return pl.pallas_call(
        functools.partial(_sb_body, tq),
        grid=(b, npair, t // tq),
        in_specs=[
            pl.BlockSpec((1, tq, LANES), lambda i, h, j: (i, j, qoff + h)),
            pl.BlockSpec((1, t, LANES), lambda i, h, j: (i, 0, koff + h)),
            pl.BlockSpec((1, t, LANES), lambda i, h, j: (i, 0, voff + h)),
            pl.BlockSpec((1, LANES), lambda i, h, j: (0, 0)),
            pl.BlockSpec((1, LANES), lambda i, h, j: (0, 0)),
            pl.BlockSpec((tq, 2 * tq), lambda i, h, j: (0, 0)),
        ],
        out_specs=pl.BlockSpec((1, tq, LANES), lambda i, h, j: (i, j, h)),
        out_shape=jax.ShapeDtypeStruct((b, t, BRANCH_W), F32),
        scratch_shapes=[pltpu.VMEM((t, LANES), BF16), pltpu.VMEM((t, LANES), BF16)],
        compiler_params=_params("parallel", "parallel", "arbitrary"),
        name="sb_attn",
    )(P, P, P, qg, kg, u)


def _memkv_body(m_ref, g_ref, w_ref, kg_ref, k_o, v_o):
    kv = _dot(_rms(m_ref[0], g_ref[...]).astype(BF16), w_ref[...])
    for h in range(MEM_HEADS):
        sl = slice(h * MEM_HD, (h + 1) * MEM_HD)
        k_o[0, :, sl] = _rms(kv[:, sl], kg_ref[...]).astype(BF16)
    v_o[0] = kv[:, BRANCH_W:].astype(BF16)


def _mem_kv(mem, g, w_bf16, kg):
    b = mem.shape[0]
    out = jax.ShapeDtypeStruct((b, MEM_LEN, BRANCH_W), BF16)
    blk = pl.BlockSpec((1, MEM_LEN, BRANCH_W), lambda i: (i, 0, 0))
    return pl.pallas_call(
        _memkv_body,
        grid=(b,),
        in_specs=[
            pl.BlockSpec((1, MEM_LEN, D_MODEL), lambda i: (i, 0, 0)),
            pl.BlockSpec((1, D_MODEL), lambda i: (0, 0)),
            pl.BlockSpec((D_MODEL, 2 * BRANCH_W), lambda i: (0, 0)),
            pl.BlockSpec((1, MEM_HD), lambda i: (0, 0)),
        ],
        out_specs=[blk, blk],
        out_shape=[out, out],
        compiler_params=_params("parallel"),
        name="mem_kv",
    )(mem, g, w_bf16, kg)


def _memattn_body(q_ref, k_ref, v_ref, qg_ref, o_ref):
    q = q_ref[0]
    for h in range(MEM_HEADS):
        sl = slice(h * MEM_HD, (h + 1) * MEM_HD)
        qh = (_rms(q[:, sl], qg_ref[...]) * (MEM_HD ** -0.5)).astype(BF16)
        s = lax.dot_general(qh, k_ref[0, :, sl], NT_DIMS, preferred_element_type=F32)
        p = jnp.exp(s - jnp.max(s, axis=-1, keepdims=True))
        den = jnp.sum(p, axis=-1, keepdims=True)
        o_ref[0, :, sl] = _dot(p.astype(BF16), v_ref[0, :, sl]) / den


def _mem_attention(P, kn, mv, qg, tq=512):
    b, t, _ = P.shape
    qoff = (RWKV_PAD + 1536) // BRANCH_W
    kv = pl.BlockSpec((1, MEM_LEN, BRANCH_W), lambda i, j: (i, 0, 0))
    return pl.pallas_call(
        _memattn_body,
        grid=(b, t // tq),
        in_specs=[
            pl.BlockSpec((1, tq, BRANCH_W), lambda i, j: (i, j, qoff)),
            kv, kv,
            pl.BlockSpec((1, MEM_HD), lambda i, j: (0, 0)),
        ],
        out_specs=pl.BlockSpec((1, tq, BRANCH_W), lambda i, j: (i, j, 0)),
        out_shape=jax.ShapeDtypeStruct((b, t, BRANCH_W), F32),
        compiler_params=_params("parallel", "parallel"),
        name="mem_attn",
    )(P, kn, mv, qg)


def _merge_body(x_ref, y_ref, bonus_ref, g_ref, yb_ref, ym_ref, g0_ref, g1_ref, g2_ref,
                lng_ref, lnb_ref, bd_ref, wbr_ref, wout_ref, o_ref):
    bd = bd_ref[...]
    y = y_ref[0]
    d = y - _dot_ones(y, bd) * (1.0 / HEAD_DIM)
    var = _dot_ones(d * d, bd) * (1.0 / HEAD_DIM)
    ya = (d * lax.rsqrt(var + LNX_EPS) * lng_ref[...] + lnb_ref[...] + bonus_ref[0]) * g_ref[0]
    m = (_sigmoid(g0_ref[0]) * _dot(ya.astype(BF16), wbr_ref[0])
         + _sigmoid(g1_ref[0]) * _dot(yb_ref[0].astype(BF16), wbr_ref[1])
         + _sigmoid(g2_ref[0]) * _dot(ym_ref[0].astype(BF16), wbr_ref[2]))
    o_ref[0] = x_ref[0] + _dot(m.astype(BF16), wout_ref[...])


def _merge(x, y, bonus, g, yb, ym, P, lng, lnb, bd, wbr, wout, tq=256):
    b, t, _ = x.shape
    tok = pl.BlockSpec((1, tq, BRANCH_W), lambda i, j: (i, j, 0))
    xblk = pl.BlockSpec((1, tq, D_MODEL), lambda i, j: (i, j, 0))
    goff = (RWKV_PAD + 2048) // D_MODEL

    def gate(n):
        return pl.BlockSpec((1, tq, D_MODEL), lambda i, j: (i, j, goff + n))

    def full(a):
        return pl.BlockSpec(a.shape, lambda i, j: (0,) * a.ndim)

    return pl.pallas_call(
        _merge_body,
        grid=(b, t // tq),
        in_specs=[xblk, tok, tok, tok, tok, tok, gate(0), gate(1), gate(2),
                  full(lng), full(lnb), full(bd), full(wbr), full(wout)],
        out_specs=xblk,
        out_shape=jax.ShapeDtypeStruct(x.shape, F32),
        compiler_params=_params("parallel", "parallel"),
        name="merge",
    )(x, y, bonus, g, yb, ym, P, P, P, lng, lnb, bd, wbr, wout)


FF_TILE = 1408


def _ffn_body(tq, x_ref, g_ref, wup_ref, cw_ref, cb_ref, wdn_ref, o_ref, carry):
    @pl.when(pl.program_id(1) == 0)
    def _():
        carry[...] = jnp.zeros_like(carry)

    x = x_ref[0]
    h = _rms(x, g_ref[...]).astype(BF16)
    row = lax.broadcasted_iota(jnp.int32, (tq, 1), 0)
    acc = jnp.zeros((tq, D_MODEL), F32)
    for f in range(D_FF // FF_TILE):
        halves = []
        for half in range(2):
            c0 = half * D_FF + f * FF_TILE
            sl = slice(c0, c0 + FF_TILE)
            hh = _dot(h, wup_ref[:, sl])
            cp = carry[:, sl]
            s1 = jnp.where(row == 0, cp[7:8], pltpu.roll(hh, 1, 0))
            s2 = jnp.where(row == 0, cp[6:7], jnp.where(row == 1, cp[7:8], pltpu.roll(hh, 2, 0)))
            carry[:, sl] = hh[tq - 8:, :]
            cw = cw_ref[:, sl]
            halves.append(cb_ref[:, sl] + s2 * cw[0:1] + s1 * cw[1:2] + hh * cw[2:3])
        gate, val = halves
        u = gate * _sigmoid(gate) * val
        acc = acc + _dot(u.astype(BF16), wdn_ref[f * FF_TILE:(f + 1) * FF_TILE, :])
    o_ref[0] = x + acc


def _ffn(x, g, wup, cw, cb, wdn, tq=256):
    b, t, _ = x.shape
    xblk = pl.BlockSpec((1, tq, D_MODEL), lambda i, j: (i, j, 0))

    def full(a):
        return pl.BlockSpec(a.shape, lambda i, j: (0,) * a.ndim)

    return pl.pallas_call(
        functools.partial(_ffn_body, tq),
        grid=(b, t // tq),
        in_specs=[xblk, full(g), full(wup), full(cw), full(cb), full(wdn)],
        out_specs=xblk,
        out_shape=jax.ShapeDtypeStruct(x.shape, F32),
        scratch_shapes=[pltpu.VMEM((8, 2 * D_FF), F32)],
        compiler_params=_params("parallel", "arbitrary"),
        name="conv_ffn",
    )(x, g, wup, cw, cb, wdn)


def _row(v):
    return v.reshape(1, -1)


def kernel(x, mem, norm1_g, w_in, shift_mu, decay_w0, decay_w2, iclr_a0, iclr_a2, gate_g2, k_k, k_a, r_k, lnx_g, lnx_b, vres_v0, vres_v1, vres_v2, sb_q_norm_g, sb_k_norm_g, mem_norm_g, w_mem_kv, mem_q_norm_g, mem_k_norm_g, w_branch, w_out, norm2_g, w_up, conv_w, conv_b, w_down):
    b, t, _ = x.shape
    assert 2 * b * N_HEADS == LANES, "recurrence layout packs (value-half, batch, head) onto the lanes"
    depth = w_in.shape[0]

    head_id = jnp.arange(BRANCH_W) // HEAD_DIM
    bd = (head_id[:, None] == head_id[None, :]).astype(BF16)
    tri = jnp.arange(LANES)
    u_ext = jnp.concatenate([(tri[:, None] > tri[None, :]).astype(BF16),
                             jnp.ones((LANES, LANES), BF16)], axis=1)
    zpad64 = jnp.zeros((64, BRANCH_W), F32)

    v_first = None
    for l in range(depth):
        w_perm = jnp.concatenate(
            [w_in[l][:, :RWKV_COLS], jnp.zeros((D_MODEL, RWKV_PAD - RWKV_COLS), F32), w_in[l][:, RWKV_COLS:]],
            axis=1).astype(BF16)
        P = _proj(x.reshape(b * t, D_MODEL), _row(norm1_g[l]), w_perm).reshape(b, t, P_COLS)

        vecs = (_row(shift_mu[l]), _row(decay_w0[l]), _row(iclr_a0[l]), _row(k_k[l]), _row(k_a[l]), _row(r_k[l]))
        mats = (jnp.concatenate([decay_w2[l], zpad64], axis=0).astype(BF16),
                jnp.concatenate([zpad64, iclr_a2[l]], axis=0).astype(BF16),
                gate_g2[l].astype(BF16))
        vres = None
        if l > 0:
            v1 = jnp.pad(vres_v1[l - 1], ((0, 0), (0, LANES - vres_v1.shape[-1]))).astype(BF16)
            v2 = jnp.pad(vres_v2[l - 1], ((0, LANES - vres_v2.shape[-2]), (0, 0))).astype(BF16)
            vres = (v_first, _row(vres_v0[l - 1]), v1, v2)
        r, dec, k2, v, nkk, bb, g, bonus = _prep(P, vecs, mats, bd, vres)
        if l == 0:
            v_first = v
        y = _recurrence(_to_key_lanes(dec), _to_key_lanes(nkk), _to_key_lanes(bb), _to_key_lanes(k2),
                        _to_key_lanes(r), _to_val_lanes(v))
        y = _from_val_lanes(y, b)

        qg2 = _row(jnp.concatenate([sb_q_norm_g[l], sb_q_norm_g[l]]))
        kg2 = _row(jnp.concatenate([sb_k_norm_g[l], sb_k_norm_g[l]]))
        yb = _sb_attention(P, qg2, kg2, u_ext)

        kn, mv = _mem_kv(mem, _row(mem_norm_g[l]), w_mem_kv[l].astype(BF16), _row(mem_k_norm_g[l]))
        ym = _mem_attention(P, kn, mv, _row(mem_q_norm_g[l]))

        x = _merge(x, y, bonus, g, yb, ym, P, _row(lnx_g[l]), _row(lnx_b[l]), bd,
                   w_branch[l].astype(BF16), w_out[l].astype(BF16))
        x = _ffn(x, _row(norm2_g[l]), w_up[l].astype(BF16), conv_w[l], _row(conv_b[l]), w_down[l].astype(BF16))
    return x
```

```python
import functools

import jax
import jax.numpy as jnp
from jax import lax
from jax.experimental import pallas as pl
from jax.experimental.pallas import tpu as pltpu

F32 = jnp.float32
BF16 = jnp.bfloat16

D_MODEL = 1024
N_HEADS = 8
HEAD_DIM = 64
BRANCH_W = 512
RWKV_COLS = 1792
RWKV_PAD = 2048
P_COLS = 7168
MEM_LEN = 256
MEM_HEADS = 4
MEM_HD = 128
D_FF = 2816
NORM_EPS = 1e-6
LNX_EPS = 64e-5
LANES = 128
MASKED_LOGIT = -1e30
SB_ROWS = 128
VMEM_LIMIT = 56 * 1024 * 1024

NT_DIMS = (((1,), (1,)), ((), ()))


def _rms(xf, g, eps=NORM_EPS):
    return xf * lax.rsqrt(jnp.mean(xf * xf, axis=-1, keepdims=True) + eps) * g


def _sigmoid(x):
    return 1.0 / (1.0 + jnp.exp(-x))


def _softplus(x):
    return jnp.maximum(x, 0.0) + jnp.log(1.0 + jnp.exp(-jnp.abs(x)))


def _dot(a, b):
    return jnp.dot(a, b, preferred_element_type=F32)


def _split3(x):
    h1 = x.astype(BF16)
    r1 = x - h1.astype(F32)
    h2 = r1.astype(BF16)
    h3 = (r1 - h2.astype(F32)).astype(BF16)
    return h1, h2, h3


def _dot_ones(x, ones_bf16):
    h1, h2, h3 = _split3(x)
    return _dot(h1, ones_bf16) + _dot(h2, ones_bf16) + _dot(h3, ones_bf16)


def _params(*sem):
    return pltpu.CompilerParams(dimension_semantics=sem, vmem_limit_bytes=VMEM_LIMIT)


def _proj_body(x_ref, g_ref, w_ref, o_ref, h_scr):
    @pl.when(pl.program_id(1) == 0)
    def _():
        h_scr[...] = _rms(x_ref[...], g_ref[...]).astype(BF16)

    o_ref[...] = _dot(h_scr[...], w_ref[...])


def _proj(x2d, g, w_bf16, tq=1024, tc=1024):
    n = x2d.shape[0]
    return pl.pallas_call(
        _proj_body,
        grid=(n // tq, P_COLS // tc),
        in_specs=[
            pl.BlockSpec((tq, D_MODEL), lambda i, j: (i, 0)),
            pl.BlockSpec((1, D_MODEL), lambda i, j: (0, 0)),
            pl.BlockSpec((D_MODEL, tc), lambda i, j: (0, j)),
        ],
        out_specs=pl.BlockSpec((tq, tc), lambda i, j: (i, j)),
        out_shape=jax.ShapeDtypeStruct((n, P_COLS), F32),
        scratch_shapes=[pltpu.VMEM((tq, D_MODEL), BF16)],
        compiler_params=_params("parallel", "arbitrary"),
        name="proj",
    )(x2d, g, w_bf16)


def _prep_body(has_vres, tq, *refs):
    if has_vres:
        (p_ref, mu_ref, w0_ref, w2_ref, a0_ref, a2_ref, g2_ref, kk_ref, ka_ref, rk_ref, bd_ref,
         vf_ref, v0_ref, v1_ref, v2_ref,
         r_o, dec_o, k_o, v_o, nkk_o, bb_o, g_o, bonus_o, carry) = refs
    else:
        (p_ref, mu_ref, w0_ref, w2_ref, a0_ref, a2_ref, g2_ref, kk_ref, ka_ref, rk_ref, bd_ref,
         r_o, dec_o, k_o, v_o, nkk_o, bb_o, g_o, bonus_o, carry) = refs

    @pl.when(pl.program_id(1) == 0)
    def _():
        carry[...] = jnp.zeros_like(carry)

    p = p_ref[0][:, :RWKV_COLS]
    row = lax.broadcasted_iota(jnp.int32, (tq, 1), 0)
    prev = jnp.where(row == 0, carry[7:8, :], pltpu.roll(p, 1, 0))
    carry[...] = p[tq - 8:, :]
    ps = p + (prev - p) * mu_ref[...]

    r = ps[:, 0:512]
    k = ps[:, 512:1024]
    v = ps[:, 1024:1536]
    wa = ps[:, 1536:1664]
    gl = ps[:, 1664:1792]

    wlog = -_softplus(-(w0_ref[...] + _dot(jnp.tanh(wa).astype(BF16), w2_ref[...]))) - 0.5
    dec = jnp.exp(-jnp.exp(wlog))
    a = _sigmoid(a0_ref[...] + _dot(wa.astype(BF16), a2_ref[...]))
    g = _dot(_sigmoid(gl).astype(BF16), g2_ref[...])
    if has_vres:
        lo = _dot(v.astype(BF16), v1_ref[...])
        v = v + (vf_ref[0] - v) * _sigmoid(v0_ref[...] + _dot(lo.astype(BF16), v2_ref[...]))
    bd = bd_ref[...]
    kk = k * kk_ref[...]
    kk = kk / jnp.maximum(jnp.sqrt(_dot_ones(kk * kk, bd)), 1e-12)
    k2 = k * (1.0 + (a - 1.0) * ka_ref[...])
    bonus = _dot_ones(r * k2 * rk_ref[...], bd) * v

    r_o[0] = r
    dec_o[0] = dec
    k_o[0] = k2
    v_o[0] = v
    nkk_o[0] = -kk
    bb_o[0] = kk * a
    g_o[0] = g
    bonus_o[0] = bonus


def _prep(P, vecs, mats, bd, vres, tq=256):
    b, t, _ = P.shape
    has_vres = vres is not None
    tok = pl.BlockSpec((1, tq, BRANCH_W), lambda i, j: (i, j, 0))

    def full(a):
        return pl.BlockSpec(a.shape, lambda i, j: (0,) * a.ndim)

    mu, w0, a0, kk, ka, rk = vecs
    w2, a2, g2 = mats
    args = [P, mu, w0, w2, a0, a2, g2, kk, ka, rk, bd]
    in_specs = [pl.BlockSpec((1, tq, RWKV_PAD), lambda i, j: (i, j, 0))] + [full(a) for a in args[1:]]
    if has_vres:
        vf, v0, v1, v2 = vres
        args += [vf, v0, v1, v2]
        in_specs += [tok, full(v0), full(v1), full(v2)]
    out = jax.ShapeDtypeStruct((b, t, BRANCH_W), F32)
    return pl.pallas_call(
        functools.partial(_prep_body, has_vres, tq),
        grid=(b, t // tq),
        in_specs=in_specs,
        out_specs=[tok] * 8,
        out_shape=[out] * 8,
        scratch_shapes=[pltpu.VMEM((8, RWKV_COLS), F32)],
        compiler_params=_params("parallel", "arbitrary"),
        name="rwkv_prep",
    )(*args)


def _rec_body(tc, w_ref, nk_ref, b_ref, k_ref, r_ref, v_ref, y_ref, s_scr):
    @pl.when(pl.program_id(0) == 0)
    def _():
        s_scr[...] = jnp.zeros_like(s_scr)

    def step(t, carry):
        w = w_ref[t]
        nk = nk_ref[t]
        bb = b_ref[t]
        kk = k_ref[t]
        rr = r_ref[t]
        for i in range(HEAD_DIM // 2):
            s = s_scr[i]
            sa = jnp.sum(s * nk, axis=0, keepdims=True)
            vi = v_ref[t, pl.ds(i, 1), :]
            sn = s * w + (sa * bb + vi * kk)
            s_scr[i] = sn
            y_ref[t, pl.ds(i, 1), :] = jnp.sum(sn * rr, axis=0, keepdims=True)
        return carry

    lax.fori_loop(0, tc, step, 0)


def _recurrence(dec, nkk, bb, k2, r, v, tc=32):
    t = dec.shape[0]
    jspec = pl.BlockSpec((tc, HEAD_DIM, LANES), lambda i: (i, 0, 0))
    ispec = pl.BlockSpec((tc, HEAD_DIM // 2, LANES), lambda i: (i, 0, 0))
    return pl.pallas_call(
        functools.partial(_rec_body, tc),
        grid=(t // tc,),
        in_specs=[jspec] * 5 + [ispec],
        out_specs=ispec,
        out_shape=jax.ShapeDtypeStruct((t, HEAD_DIM // 2, LANES), F32),
        scratch_shapes=[pltpu.VMEM((HEAD_DIM // 2, HEAD_DIM, LANES), F32)],
        compiler_params=_params("arbitrary"),
        name="rwkv_rec",
    )(dec, nkk, bb, k2, r, v)


def _to_key_lanes(a):
    b, t, _ = a.shape
    a = a.reshape(b, t, N_HEADS, HEAD_DIM).transpose(1, 3, 0, 2).reshape(t, HEAD_DIM, b * N_HEADS)
    return jnp.concatenate([a, a], axis=-1)


def _to_val_lanes(a):
    b, t, _ = a.shape
    a = a.reshape(b, t, N_HEADS, 2, HEAD_DIM // 2).transpose(1, 4, 3, 0, 2)
    return a.reshape(t, HEAD_DIM // 2, 2 * b * N_HEADS)


def _from_val_lanes(y, b):
    t = y.shape[0]
    y = y.reshape(t, HEAD_DIM // 2, 2, b, N_HEADS).transpose(3, 0, 4, 2, 1)
    return y.reshape(b, t, BRANCH_W)


def _sb_body(tq, tk, q_ref, k_ref, v_ref, qg_ref, kg_ref, u_ref, o_ref,
             kn_scr, vb_scr, qn_scr, hl_scr, lsig_scr, logit_scr, run_scr, acc_scr):
    qi = pl.program_id(2)
    nsub = tq // tk
    nblk = kn_scr.shape[0]
    lane = lax.broadcasted_iota(jnp.int32, (1, LANES), 1)
    first = lane < HEAD_DIM
    m_a = first.astype(F32)
    m_b = 1.0 - m_a

    def headnorm(x, g):
        x2 = x * x
        s_a = jnp.sum(x2 * m_a, axis=-1, keepdims=True)
        s_b = jnp.sum(x2 * m_b, axis=-1, keepdims=True)
        ms = jnp.where(first, s_a, s_b) * (1.0 / HEAD_DIM)
        return x * lax.rsqrt(ms + NORM_EPS) * g

    @pl.when(qi == 0)
    def _():
        kn = headnorm(k_ref[0], kg_ref[...])
        v = v_ref[0]
        kn_scr[:, :tk, :] = (kn * m_a).astype(BF16).reshape(nblk, tk, LANES)
        kn_scr[:, tk:, :] = (kn * m_b).astype(BF16).reshape(nblk, tk, LANES)
        vb_scr[:, :tk, :] = (v * m_a).astype(BF16).reshape(nblk, tk, LANES)
        vb_scr[:, tk:, :] = (v * m_b).astype(BF16).reshape(nblk, tk, LANES)

    qn_scr[...] = (headnorm(q_ref[0], qg_ref[...]) * (HEAD_DIM ** -0.5)).astype(BF16)
    run_scr[...] = jnp.zeros_like(run_scr)
    acc_scr[...] = jnp.zeros_like(acc_scr)
    uu = u_ref[...]
    rowi = lax.broadcasted_iota(jnp.int32, (SB_ROWS, 2 * tk), 0)
    coli = lax.broadcasted_iota(jnp.int32, (SB_ROWS, 2 * tk), 1) & (tk - 1)

    last = qi * nsub + (nsub - 1)
    heads = (slice(0, tk), slice(tk, 2 * tk))
    groups = [slice(s * SB_ROWS, (s + 1) * SB_ROWS) for s in range(tq // SB_ROWS)]

    def score(m, masked, rows):
        z = lax.dot_general(qn_scr[rows, :], kn_scr[jnp.maximum(last - m, 0)], NT_DIMS,
                            preferred_element_type=F32)
        l1m = -_softplus(z)
        lsig = z + l1m
        if masked:
            mask = (coli + ((nsub - 1 - m) * tk - rows.start)) < rowi
            l1m = jnp.where(mask, l1m, 0.0)
            lsig = jnp.where(mask, lsig, MASKED_LOGIT)
        hi = l1m.astype(BF16)
        lo = (l1m - hi.astype(F32)).astype(BF16)
        for h, sl in enumerate(heads):
            hl_scr[h, rows, :] = jnp.concatenate([hi[:, sl], lo[:, sl]], axis=1)
        lsig_scr[rows, :] = lsig

    def suffix(rows):
        for h, sl in enumerate(heads):
            ce = _dot(hl_scr[h, rows, :], uu)
            run = run_scr[h, rows, :]
            logit_scr[rows, sl] = lsig_scr[rows, sl] + ce[:, :tk] + run
            run_scr[h, rows, :] = run + ce[:, tk:]

    def weigh(m, rows):
        acc_scr[rows, :] += _dot(jnp.exp(logit_scr[rows, :]).astype(BF16), vb_scr[last - m])

    def body(masked, m, carry):
        for rows in groups:
            weigh(m, rows)
            suffix(rows)
            score(m + 2, masked, rows)
        return carry

    for rows in groups:
        score(0, True, rows)
    for rows in groups:
        suffix(rows)
        score(1, True, rows)
    lax.fori_loop(0, nsub, functools.partial(body, True), 0)
    lax.fori_loop(nsub, (qi + 1) * nsub, functools.partial(body, False), 0)
    o_ref[0] = acc_scr[...]


def _sb_attention(P, qg, kg, uu, tq=512, tk=LANES):
    b, t, _ = P.shape
    npair = N_HEADS // 2
    qoff, koff, voff = (RWKV_PAD // LANES, (RWKV_PAD + 512) // LANES, (RWKV_PAD + 1024) // LANES)
    stacked = pltpu.VMEM((t // tk, 2 * tk, LANES), BF16)
    return pl.pallas_call(
        functools.partial(_sb_body, tq, tk),
        grid=(b, npair, t // tq),
        in_specs=[
            pl.BlockSpec((1, tq, LANES), lambda i, h, j: (i, j, qoff + h)),
            pl.BlockSpec((1, t, LANES), lambda i, h, j: (i, 0, koff + h)),
            pl.BlockSpec((1, t, LANES), lambda i, h, j: (i, 0, voff + h)),
            pl.BlockSpec((1, LANES), lambda i, h, j: (0, 0)),
            pl.BlockSpec((1, LANES), lambda i, h, j: (0, 0)),
            pl.BlockSpec((2 * tk, 2 * tk), lambda i, h, j: (0, 0)),
        ],
        out_specs=pl.BlockSpec((1, tq, LANES), lambda i, h, j: (i, j, h)),
        out_shape=jax.ShapeDtypeStruct((b, t, BRANCH_W), F32),
        scratch_shapes=[stacked, stacked,
                        pltpu.VMEM((tq, LANES), BF16),
                        pltpu.VMEM((2, tq, 2 * tk), BF16),
                        pltpu.VMEM((tq, 2 * tk), F32),
                        pltpu.VMEM((tq, 2 * tk), F32),
                        pltpu.VMEM((2, tq, LANES), F32),
                        pltpu.VMEM((tq, LANES), F32)],
        compiler_params=_params("parallel", "parallel", "arbitrary"),
        name="sb_attn",
    )(P, P, P, qg, kg, uu)


def _memkv_body(m_ref, g_ref, w_ref, kg_ref, k_o, v_o):
    kv = _dot(_rms(m_ref[0], g_ref[...]).astype(BF16), w_ref[...])
    for h in range(MEM_HEADS):
        sl = slice(h * MEM_HD, (h + 1) * MEM_HD)
        k_o[0, :, sl] = _rms(kv[:, sl], kg_ref[...]).astype(BF16)
    v_o[0] = kv[:, BRANCH_W:].astype(BF16)


def _mem_kv(mem, g, w_bf16, kg):
    b = mem.shape[0]
    out = jax.ShapeDtypeStruct((b, MEM_LEN, BRANCH_W), BF16)
    blk = pl.BlockSpec((1, MEM_LEN, BRANCH_W), lambda i: (i, 0, 0))
    return pl.pallas_call(
        _memkv_body,
        grid=(b,),
        in_specs=[
            pl.BlockSpec((1, MEM_LEN, D_MODEL), lambda i: (i, 0, 0)),
            pl.BlockSpec((1, D_MODEL), lambda i: (0, 0)),
            pl.BlockSpec((D_MODEL, 2 * BRANCH_W), lambda i: (0, 0)),
            pl.BlockSpec((1, MEM_HD), lambda i: (0, 0)),
        ],
        out_specs=[blk, blk],
        out_shape=[out, out],
        compiler_params=_params("parallel"),
        name="mem_kv",
    )(mem, g, w_bf16, kg)


def _memattn_body(q_ref, k_ref, v_ref, qg_ref, o_ref):
    q = q_ref[0]
    for h in range(MEM_HEADS):
        sl = slice(h * MEM_HD, (h + 1) * MEM_HD)
        qh = (_rms(q[:, sl], qg_ref[...]) * (MEM_HD ** -0.5)).astype(BF16)
        s = lax.dot_general(qh, k_ref[0, :, sl], NT_DIMS, preferred_element_type=F32)
        p = jnp.exp(s - jnp.max(s, axis=-1, keepdims=True))
        den = jnp.sum(p, axis=-1, keepdims=True)
        o_ref[0, :, sl] = _dot(p.astype(BF16), v_ref[0, :, sl]) / den


def _mem_attention(P, kn, mv, qg, tq=512):
    b, t, _ = P.shape
    qoff = (RWKV_PAD + 1536) // BRANCH_W
    kv = pl.BlockSpec((1, MEM_LEN, BRANCH_W), lambda i, j: (i, 0, 0))
    return pl.pallas_call(
        _memattn_body,
        grid=(b, t // tq),
        in_specs=[
            pl.BlockSpec((1, tq, BRANCH_W), lambda i, j: (i, j, qoff)),
            kv, kv,
            pl.BlockSpec((1, MEM_HD), lambda i, j: (0, 0)),
        ],
        out_specs=pl.BlockSpec((1, tq, BRANCH_W), lambda i, j: (i, j, 0)),
        out_shape=jax.ShapeDtypeStruct((b, t, BRANCH_W), F32),
        compiler_params=_params("parallel", "parallel"),
        name="mem_attn",
    )(P, kn, mv, qg)


def _merge_body(x_ref, y_ref, bonus_ref, g_ref, yb_ref, ym_ref, g0_ref, g1_ref, g2_ref,
                lng_ref, lnb_ref, bd_ref, wbr_ref, wout_ref, o_ref):
    bd = bd_ref[...]
    y = y_ref[0]
    d = y - _dot_ones(y, bd) * (1.0 / HEAD_DIM)
    var = _dot_ones(d * d, bd) * (1.0 / HEAD_DIM)
    ya = (d * lax.rsqrt(var + LNX_EPS) * lng_ref[...] + lnb_ref[...] + bonus_ref[0]) * g_ref[0]
    m = (_sigmoid(g0_ref[0]) * _dot(ya.astype(BF16), wbr_ref[0])
         + _sigmoid(g1_ref[0]) * _dot(yb_ref[0].astype(BF16), wbr_ref[1])
         + _sigmoid(g2_ref[0]) * _dot(ym_ref[0].astype(BF16), wbr_ref[2]))
    o_ref[0] = x_ref[0] + _dot(m.astype(BF16), wout_ref[...])


def _merge(x, y, bonus, g, yb, ym, P, lng, lnb, bd, wbr, wout, tq=256):
    b, t, _ = x.shape
    tok = pl.BlockSpec((1, tq, BRANCH_W), lambda i, j: (i, j, 0))
    xblk = pl.BlockSpec((1, tq, D_MODEL), lambda i, j: (i, j, 0))
    goff = (RWKV_PAD + 2048) // D_MODEL

    def gate(n):
        return pl.BlockSpec((1, tq, D_MODEL), lambda i, j: (i, j, goff + n))

    def full(a):
        return pl.BlockSpec(a.shape, lambda i, j: (0,) * a.ndim)

    return pl.pallas_call(
        _merge_body,
        grid=(b, t // tq),
        in_specs=[xblk, tok, tok, tok, tok, tok, gate(0), gate(1), gate(2),
                  full(lng), full(lnb), full(bd), full(wbr), full(wout)],
        out_specs=xblk,
        out_shape=jax.ShapeDtypeStruct(x.shape, F32),
        compiler_params=_params("parallel", "parallel"),
        name="merge",
    )(x, y, bonus, g, yb, ym, P, P, P, lng, lnb, bd, wbr, wout)


FF_TILE = 1408


def _ffn_body(tq, x_ref, g_ref, wup_ref, cw_ref, cb_ref, wdn_ref, o_ref, carry):
    @pl.when(pl.program_id(1) == 0)
    def _():
        carry[...] = jnp.zeros_like(carry)

    x = x_ref[0]
    h = _rms(x, g_ref[...]).astype(BF16)
    row = lax.broadcasted_iota(jnp.int32, (tq, 1), 0)
    acc = jnp.zeros((tq, D_MODEL), F32)
    for f in range(D_FF // FF_TILE):
        halves = []
        for half in range(2):
            c0 = half * D_FF + f * FF_TILE
            sl = slice(c0, c0 + FF_TILE)
            hh = _dot(h, wup_ref[:, sl])
            cp = carry[:, sl]
            s1 = jnp.where(row == 0, cp[7:8], pltpu.roll(hh, 1, 0))
            s2 = jnp.where(row == 0, cp[6:7], jnp.where(row == 1, cp[7:8], pltpu.roll(hh, 2, 0)))
            carry[:, sl] = hh[tq - 8:, :]
            cw = cw_ref[:, sl]
            halves.append(cb_ref[:, sl] + s2 * cw[0:1] + s1 * cw[1:2] + hh * cw[2:3])
        gate, val = halves
        u = gate * _sigmoid(gate) * val
        acc = acc + _dot(u.astype(BF16), wdn_ref[f * FF_TILE:(f + 1) * FF_TILE, :])
    o_ref[0] = x + acc


def _ffn(x, g, wup, cw, cb, wdn, tq=256):
    b, t, _ = x.shape
    xblk = pl.BlockSpec((1, tq, D_MODEL), lambda i, j: (i, j, 0))

    def full(a):
        return pl.BlockSpec(a.shape, lambda i, j: (0,) * a.ndim)

    return pl.pallas_call(
        functools.partial(_ffn_body, tq),
        grid=(b, t // tq),
        in_specs=[xblk, full(g), full(wup), full(cw), full(cb), full(wdn)],
        out_specs=xblk,
        out_shape=jax.ShapeDtypeStruct(x.shape, F32),
        scratch_shapes=[pltpu.VMEM((8, 2 * D_FF), F32)],
        compiler_params=_params("parallel", "arbitrary"),
        name="conv_ffn",
    )(x, g, wup, cw, cb, wdn)


def _row(v):
    return v.reshape(1, -1)


def kernel(x, mem, norm1_g, w_in, shift_mu, decay_w0, decay_w2, iclr_a0, iclr_a2, gate_g2, k_k, k_a, r_k, lnx_g, lnx_b, vres_v0, vres_v1, vres_v2, sb_q_norm_g, sb_k_norm_g, mem_norm_g, w_mem_kv, mem_q_norm_g, mem_k_norm_g, w_branch, w_out, norm2_g, w_up, conv_w, conv_b, w_down):
    b, t, _ = x.shape
    assert 2 * b * N_HEADS == LANES, "recurrence layout packs (value-half, batch, head) onto the lanes"
    depth = w_in.shape[0]

    head_id = jnp.arange(BRANCH_W) // HEAD_DIM
    bd = (head_id[:, None] == head_id[None, :]).astype(BF16)
    tri = jnp.arange(LANES)
    u_ext = jnp.concatenate([(tri[:, None] > tri[None, :]).astype(BF16),
                             jnp.ones((LANES, LANES), BF16)], axis=1)
    u_ext = jnp.concatenate([u_ext, u_ext], axis=0)
    zpad64 = jnp.zeros((64, BRANCH_W), F32)

    v_first = None
    for l in range(depth):
        w_perm = jnp.concatenate(
            [w_in[l][:, :RWKV_COLS], jnp.zeros((D_MODEL, RWKV_PAD - RWKV_COLS), F32), w_in[l][:, RWKV_COLS:]],
            axis=1).astype(BF16)
        P = _proj(x.reshape(b * t, D_MODEL), _row(norm1_g[l]), w_perm).reshape(b, t, P_COLS)

        vecs = (_row(shift_mu[l]), _row(decay_w0[l]), _row(iclr_a0[l]), _row(k_k[l]), _row(k_a[l]), _row(r_k[l]))
        mats = (jnp.concatenate([decay_w2[l], zpad64], axis=0).astype(BF16),
                jnp.concatenate([zpad64, iclr_a2[l]], axis=0).astype(BF16),
                gate_g2[l].astype(BF16))
        vres = None
        if l > 0:
            v1 = jnp.pad(vres_v1[l - 1], ((0, 0), (0, LANES - vres_v1.shape[-1]))).astype(BF16)
            v2 = jnp.pad(vres_v2[l - 1], ((0, LANES - vres_v2.shape[-2]), (0, 0))).astype(BF16)
            vres = (v_first, _row(vres_v0[l - 1]), v1, v2)
        r, dec, k2, v, nkk, bb, g, bonus = _prep(P, vecs, mats, bd, vres)
        if l == 0:
            v_first = v
        y = _recurrence(_to_key_lanes(dec), _to_key_lanes(nkk), _to_key_lanes(bb), _to_key_lanes(k2),
                        _to_key_lanes(r), _to_val_lanes(v))
        y = _from_val_lanes(y, b)

        qg2 = _row(jnp.concatenate([sb_q_norm_g[l], sb_q_norm_g[l]]))
        kg2 = _row(jnp.concatenate([sb_k_norm_g[l], sb_k_norm_g[l]]))
        yb = _sb_attention(P, qg2, kg2, u_ext)

        kn, mv = _mem_kv(mem, _row(mem_norm_g[l]), w_mem_kv[l].astype(BF16), _row(mem_k_norm_g[l]))
        ym = _mem_attention(P, kn, mv, _row(mem_q_norm_g[l]))

        x = _merge(x, y, bonus, g, yb, ym, P, _row(lnx_g[l]), _row(lnx_b[l]), bd,
                   w_branch[l].astype(BF16), w_out[l].astype(BF16))
        x = _ffn(x, _row(norm2_g[l]), w_up[l].astype(BF16), conv_w[l], _row(conv_b[l]), w_down[l].astype(BF16))
    return x
```

```python
import functools

import jax
import jax.numpy as jnp
from jax import lax
from jax.experimental import pallas as pl
from jax.experimental.pallas import tpu as pltpu

F32 = jnp.float32
BF16 = jnp.bfloat16

D_MODEL = 1024
N_HEADS = 8
HEAD_DIM = 64
BRANCH_W = 512
RWKV_COLS = 1792
RWKV_PAD = 2048
P_COLS = 7168
MEM_LEN = 256
MEM_HEADS = 4
MEM_HD = 128
D_FF = 2816
NORM_EPS = 1e-6
LNX_EPS = 64e-5
LANES = 128
MASKED_LOGIT = -1e30
SB_ROWS = 128
VMEM_LIMIT = 56 * 1024 * 1024

NT_DIMS = (((1,), (1,)), ((), ()))


def _rms(xf, g, eps=NORM_EPS):
    return xf * lax.rsqrt(jnp.mean(xf * xf, axis=-1, keepdims=True) + eps) * g


def _sigmoid(x):
    return 1.0 / (1.0 + jnp.exp(-x))


def _softplus(x):
    return jnp.maximum(x, 0.0) + jnp.log(1.0 + jnp.exp(-jnp.abs(x)))


def _dot(a, b):
    return jnp.dot(a, b, preferred_element_type=F32)


def _split3(x):
    h1 = x.astype(BF16)
    r1 = x - h1.astype(F32)
    h2 = r1.astype(BF16)
    h3 = (r1 - h2.astype(F32)).astype(BF16)
    return h1, h2, h3


def _dot_ones(x, ones_bf16):
    h1, h2, h3 = _split3(x)
    return _dot(h1, ones_bf16) + _dot(h2, ones_bf16) + _dot(h3, ones_bf16)


def _params(*sem):
    return pltpu.CompilerParams(dimension_semantics=sem, vmem_limit_bytes=VMEM_LIMIT)


def _proj_body(x_ref, g_ref, w_ref, o_ref, h_scr):
    @pl.when(pl.program_id(1) == 0)
    def _():
        h_scr[...] = _rms(x_ref[...], g_ref[...]).astype(BF16)

    o_ref[...] = _dot(h_scr[...], w_ref[...])


def _proj(x2d, g, w_bf16, tq=1024, tc=1024):
    n = x2d.shape[0]
    return pl.pallas_call(
        _proj_body,
        grid=(n // tq, P_COLS // tc),
        in_specs=[
            pl.BlockSpec((tq, D_MODEL), lambda i, j: (i, 0)),
            pl.BlockSpec((1, D_MODEL), lambda i, j: (0, 0)),
            pl.BlockSpec((D_MODEL, tc), lambda i, j: (0, j)),
        ],
        out_specs=pl.BlockSpec((tq, tc), lambda i, j: (i, j)),
        out_shape=jax.ShapeDtypeStruct((n, P_COLS), F32),
        scratch_shapes=[pltpu.VMEM((tq, D_MODEL), BF16)],
        compiler_params=_params("parallel", "arbitrary"),
        name="proj",
    )(x2d, g, w_bf16)


def _prep_body(has_vres, tq, *refs):
    if has_vres:
        (p_ref, mu_ref, w0_ref, w2_ref, a0_ref, a2_ref, g2_ref, kk_ref, ka_ref, rk_ref, bd_ref,
         vf_ref, v0_ref, v1_ref, v2_ref,
         r_o, dec_o, k_o, v_o, nkk_o, bb_o, g_o, bonus_o, carry) = refs
    else:
        (p_ref, mu_ref, w0_ref, w2_ref, a0_ref, a2_ref, g2_ref, kk_ref, ka_ref, rk_ref, bd_ref,
         r_o, dec_o, k_o, v_o, nkk_o, bb_o, g_o, bonus_o, carry) = refs

    @pl.when(pl.program_id(1) == 0)
    def _():
        carry[...] = jnp.zeros_like(carry)

    p = p_ref[0][:, :RWKV_COLS]
    row = lax.broadcasted_iota(jnp.int32, (tq, 1), 0)
    prev = jnp.where(row == 0, carry[7:8, :], pltpu.roll(p, 1, 0))
    carry[...] = p[tq - 8:, :]
    ps = p + (prev - p) * mu_ref[...]

    r = ps[:, 0:512]
    k = ps[:, 512:1024]
    v = ps[:, 1024:1536]
    wa = ps[:, 1536:1664]
    gl = ps[:, 1664:1792]

    wlog = -_softplus(-(w0_ref[...] + _dot(jnp.tanh(wa).astype(BF16), w2_ref[...]))) - 0.5
    dec = jnp.exp(-jnp.exp(wlog))
    a = _sigmoid(a0_ref[...] + _dot(wa.astype(BF16), a2_ref[...]))
    g = _dot(_sigmoid(gl).astype(BF16), g2_ref[...])
    if has_vres:
        lo = _dot(v.astype(BF16), v1_ref[...])
        v = v + (vf_ref[0] - v) * _sigmoid(v0_ref[...] + _dot(lo.astype(BF16), v2_ref[...]))
    bd = bd_ref[...]
    kk = k * kk_ref[...]
    kk = kk / jnp.maximum(jnp.sqrt(_dot_ones(kk * kk, bd)), 1e-12)
    k2 = k * (1.0 + (a - 1.0) * ka_ref[...])
    bonus = _dot_ones(r * k2 * rk_ref[...], bd) * v

    r_o[0] = r
    dec_o[0] = dec
    k_o[0] = k2
    v_o[0] = v
    nkk_o[0] = -kk
    bb_o[0] = kk * a
    g_o[0] = g
    bonus_o[0] = bonus


def _prep(P, vecs, mats, bd, vres, tq=256):
    b, t, _ = P.shape
    has_vres = vres is not None
    tok = pl.BlockSpec((1, tq, BRANCH_W), lambda i, j: (i, j, 0))

    def full(a):
        return pl.BlockSpec(a.shape, lambda i, j: (0,) * a.ndim)

    mu, w0, a0, kk, ka, rk = vecs
    w2, a2, g2 = mats
    args = [P, mu, w0, w2, a0, a2, g2, kk, ka, rk, bd]
    in_specs = [pl.BlockSpec((1, tq, RWKV_PAD), lambda i, j: (i, j, 0))] + [full(a) for a in args[1:]]
    if has_vres:
        vf, v0, v1, v2 = vres
        args += [vf, v0, v1, v2]
        in_specs += [tok, full(v0), full(v1), full(v2)]
    out = jax.ShapeDtypeStruct((b, t, BRANCH_W), F32)
    return pl.pallas_call(
        functools.partial(_prep_body, has_vres, tq),
        grid=(b, t // tq),
        in_specs=in_specs,
        out_specs=[tok] * 8,
        out_shape=[out] * 8,
        scratch_shapes=[pltpu.VMEM((8, RWKV_COLS), F32)],
        compiler_params=_params("parallel", "arbitrary"),
        name="rwkv_prep",
    )(*args)


HALF_J = HEAD_DIM // 2
SUBLANES = 8


def _rec_body(tc, w_ref, nk_ref, b_ref, k_ref, r_ref, v_ref, y_ref, s_scr):
    @pl.when(pl.program_id(0) == 0)
    def _():
        s_scr[...] = jnp.zeros_like(s_scr)

    groups = [pl.ds(g * SUBLANES, SUBLANES) for g in range(HEAD_DIM // SUBLANES)]

    def fold(x):
        return x + pltpu.roll(x, HEAD_DIM, 1)

    def step(t, carry):
        part = [None] * len(groups)
        for j in range(HALF_J):
            nk = nk_ref[t, pl.ds(j, 1), :]
            for g, rows in enumerate(groups):
                term = s_scr[j, rows, :] * nk
                part[g] = term if part[g] is None else part[g] + term
        sa = [fold(p) for p in part]
        v = [v_ref[t, rows, :] for rows in groups]
        yacc = [None] * len(groups)
        for j in range(HALF_J):
            w = w_ref[t, pl.ds(j, 1), :]
            bb = b_ref[t, pl.ds(j, 1), :]
            kk = k_ref[t, pl.ds(j, 1), :]
            rr = r_ref[t, pl.ds(j, 1), :]
            for g, rows in enumerate(groups):
                sn = s_scr[j, rows, :] * w + (sa[g] * bb + v[g] * kk)
                s_scr[j, rows, :] = sn
                term = sn * rr
                yacc[g] = term if yacc[g] is None else yacc[g] + term
        for g, rows in enumerate(groups):
            y_ref[t, rows, :] = yacc[g]
        return carry

    lax.fori_loop(0, tc, step, 0)
    y = y_ref[...].reshape(tc * HEAD_DIM, LANES)
    y_ref[...] = fold(y).reshape(tc, HEAD_DIM, LANES)


def _recurrence(dec, nkk, bb, k2, r, v, tc=32):
    t = dec.shape[0]
    jspec = pl.BlockSpec((tc, HALF_J, LANES), lambda i: (i, 0, 0))
    ispec = pl.BlockSpec((tc, HEAD_DIM, LANES), lambda i: (i, 0, 0))
    return pl.pallas_call(
        functools.partial(_rec_body, tc),
        grid=(t // tc,),
        in_specs=[jspec] * 5 + [ispec],
        out_specs=ispec,
        out_shape=jax.ShapeDtypeStruct((t, HEAD_DIM, LANES), F32),
        scratch_shapes=[pltpu.VMEM((HALF_J, HEAD_DIM, LANES), F32)],
        compiler_params=_params("arbitrary"),
        name="rwkv_rec",
    )(dec, nkk, bb, k2, r, v)


def _to_val_lanes(a):
    b, t, _ = a.shape
    a = a.reshape(b, t, N_HEADS, HEAD_DIM).transpose(1, 3, 0, 2).reshape(t, HEAD_DIM, b * N_HEADS)
    return jnp.concatenate([a, a], axis=-1)


def _to_key_lanes(a):
    b, t, _ = a.shape
    a = a.reshape(b, t, N_HEADS, 2, HALF_J).transpose(1, 4, 3, 0, 2)
    return a.reshape(t, HALF_J, 2 * b * N_HEADS)


def _from_val_lanes(y, b):
    t = y.shape[0]
    y = y[:, :, :b * N_HEADS].reshape(t, HEAD_DIM, b, N_HEADS).transpose(2, 0, 3, 1)
    return y.reshape(b, t, BRANCH_W)


def _sb_body(tq, tk, q_ref, k_ref, v_ref, qg_ref, kg_ref, u_ref, o_ref,
             kn_scr, vb_scr, qn_scr, hl_scr, lsig_scr, logit_scr, run_scr, acc_scr):
    qi = pl.program_id(2)
    nsub = tq // tk
    nblk = kn_scr.shape[0]
    lane = lax.broadcasted_iota(jnp.int32, (1, LANES), 1)
    first = lane < HEAD_DIM
    m_a = first.astype(F32)
    m_b = 1.0 - m_a

    def headnorm(x, g):
        x2 = x * x
        s_a = jnp.sum(x2 * m_a, axis=-1, keepdims=True)
        s_b = jnp.sum(x2 * m_b, axis=-1, keepdims=True)
        ms = jnp.where(first, s_a, s_b) * (1.0 / HEAD_DIM)
        return x * lax.rsqrt(ms + NORM_EPS) * g

    @pl.when(qi == 0)
    def _():
        kn = headnorm(k_ref[0], kg_ref[...])
        v = v_ref[0]
        kn_scr[:, :tk, :] = (kn * m_a).astype(BF16).reshape(nblk, tk, LANES)
        kn_scr[:, tk:, :] = (kn * m_b).astype(BF16).reshape(nblk, tk, LANES)
        vb_scr[:, :tk, :] = (v * m_a).astype(BF16).reshape(nblk, tk, LANES)
        vb_scr[:, tk:, :] = (v * m_b).astype(BF16).reshape(nblk, tk, LANES)

    qn_scr[...] = (headnorm(q_ref[0], qg_ref[...]) * (HEAD_DIM ** -0.5)).astype(BF16)
    run_scr[...] = jnp.zeros_like(run_scr)
    acc_scr[...] = jnp.zeros_like(acc_scr)
    uu = u_ref[...]
    rowi = lax.broadcasted_iota(jnp.int32, (SB_ROWS, 2 * tk), 0)
    coli = lax.broadcasted_iota(jnp.int32, (SB_ROWS, 2 * tk), 1) & (tk - 1)

    last = qi * nsub + (nsub - 1)
    heads = (slice(0, tk), slice(tk, 2 * tk))
    groups = [slice(s * SB_ROWS, (s + 1) * SB_ROWS) for s in range(tq // SB_ROWS)]

    def score(m, masked, rows):
        z = lax.dot_general(qn_scr[rows, :], kn_scr[jnp.maximum(last - m, 0)], NT_DIMS,
                            preferred_element_type=F32)
        l1m = -_softplus(z)
        lsig = z + l1m
        if masked:
            mask = (coli + ((nsub - 1 - m) * tk - rows.start)) < rowi
            l1m = jnp.where(mask, l1m, 0.0)
            lsig = jnp.where(mask, lsig, MASKED_LOGIT)
        hi = l1m.astype(BF16)
        lo = (l1m - hi.astype(F32)).astype(BF16)
        for h, sl in enumerate(heads):
            hl_scr[h, rows, :] = jnp.concatenate([hi[:, sl], lo[:, sl]], axis=1)
        lsig_scr[rows, :] = lsig

    def suffix(rows):
        for h, sl in enumerate(heads):
            ce = _dot(hl_scr[h, rows, :], uu)
            run = run_scr[h, rows, :]
            logit_scr[rows, sl] = lsig_scr[rows, sl] + ce[:, :tk] + run
            run_scr[h, rows, :] = run + ce[:, tk:]

    def weigh(m, rows):
        acc_scr[rows, :] += _dot(jnp.exp(logit_scr[rows, :]).astype(BF16), vb_scr[last - m])

    def body(m, carry):
        for rows in groups:
            weigh(m, rows)
            suffix(rows)
            score(m + 2, False, rows)
        return carry

    def live(rows, m):
        return (nsub - 1 - m) * tk <= rows.stop - 2

    for rows in groups:
        if live(rows, 0):
            score(0, True, rows)
    for rows in groups:
        if live(rows, 0):
            suffix(rows)
        if live(rows, 1):
            score(1, True, rows)
    for m in range(nsub):
        for rows in groups:
            if live(rows, m):
                weigh(m, rows)
            if live(rows, m + 1):
                suffix(rows)
            if live(rows, m + 2):
                score(m + 2, m + 2 < nsub, rows)
    lax.fori_loop(nsub, (qi + 1) * nsub, body, 0)
    o_ref[0] = acc_scr[...]


def _sb_attention(P, qg, kg, uu, tq=512, tk=LANES):
    b, t, _ = P.shape
    npair = N_HEADS // 2
    qoff, koff, voff = (RWKV_PAD // LANES, (RWKV_PAD + 512) // LANES, (RWKV_PAD + 1024) // LANES)
    stacked = pltpu.VMEM((t // tk, 2 * tk, LANES), BF16)
    return pl.pallas_call(
        functools.partial(_sb_body, tq, tk),
        grid=(b, npair, t // tq),
        in_specs=[
            pl.BlockSpec((1, tq, LANES), lambda i, h, j: (i, j, qoff + h)),
            pl.BlockSpec((1, t, LANES), lambda i, h, j: (i, 0, koff + h)),
            pl.BlockSpec((1, t, LANES), lambda i, h, j: (i, 0, voff + h)),
            pl.BlockSpec((1, LANES), lambda i, h, j: (0, 0)),
            pl.BlockSpec((1, LANES), lambda i, h, j: (0, 0)),
            pl.BlockSpec((2 * tk, 2 * tk), lambda i, h, j: (0, 0)),
        ],
        out_specs=pl.BlockSpec((1, tq, LANES), lambda i, h, j: (i, j, h)),
        out_shape=jax.ShapeDtypeStruct((b, t, BRANCH_W), F32),
        scratch_shapes=[stacked, stacked,
                        pltpu.VMEM((tq, LANES), BF16),
                        pltpu.VMEM((2, tq, 2 * tk), BF16),
                        pltpu.VMEM((tq, 2 * tk), F32),
                        pltpu.VMEM((tq, 2 * tk), F32),
                        pltpu.VMEM((2, tq, LANES), F32),
                        pltpu.VMEM((tq, LANES), F32)],
        compiler_params=_params("parallel", "parallel", "arbitrary"),
        name="sb_attn",
    )(P, P, P, qg, kg, uu)


def _memkv_body(m_ref, g_ref, w_ref, kg_ref, k_o, v_o):
    kv = _dot(_rms(m_ref[0], g_ref[...]).astype(BF16), w_ref[...])
    for h in range(MEM_HEADS):
        sl = slice(h * MEM_HD, (h + 1) * MEM_HD)
        k_o[0, :, sl] = _rms(kv[:, sl], kg_ref[...]).astype(BF16)
    v_o[0] = kv[:, BRANCH_W:].astype(BF16)


def _mem_kv(mem, g, w_bf16, kg):
    b = mem.shape[0]
    out = jax.ShapeDtypeStruct((b, MEM_LEN, BRANCH_W), BF16)
    blk = pl.BlockSpec((1, MEM_LEN, BRANCH_W), lambda i: (i, 0, 0))
    return pl.pallas_call(
        _memkv_body,
        grid=(b,),
        in_specs=[
            pl.BlockSpec((1, MEM_LEN, D_MODEL), lambda i: (i, 0, 0)),
            pl.BlockSpec((1, D_MODEL), lambda i: (0, 0)),
            pl.BlockSpec((D_MODEL, 2 * BRANCH_W), lambda i: (0, 0)),
            pl.BlockSpec((1, MEM_HD), lambda i: (0, 0)),
        ],
        out_specs=[blk, blk],
        out_shape=[out, out],
        compiler_params=_params("parallel"),
        name="mem_kv",
    )(mem, g, w_bf16, kg)


def _memattn_body(q_ref, k_ref, v_ref, qg_ref, o_ref):
    q = q_ref[0]
    for h in range(MEM_HEADS):
        sl = slice(h * MEM_HD, (h + 1) * MEM_HD)
        qh = (_rms(q[:, sl], qg_ref[...]) * (MEM_HD ** -0.5)).astype(BF16)
        s = lax.dot_general(qh, k_ref[0, :, sl], NT_DIMS, preferred_element_type=F32)
        p = jnp.exp(s - jnp.max(s, axis=-1, keepdims=True))
        den = jnp.sum(p, axis=-1, keepdims=True)
        o_ref[0, :, sl] = _dot(p.astype(BF16), v_ref[0, :, sl]) / den


def _mem_attention(P, kn, mv, qg, tq=512):
    b, t, _ = P.shape
    qoff = (RWKV_PAD + 1536) // BRANCH_W
    kv = pl.BlockSpec((1, MEM_LEN, BRANCH_W), lambda i, j: (i, 0, 0))
    return pl.pallas_call(
        _memattn_body,
        grid=(b, t // tq),
        in_specs=[
            pl.BlockSpec((1, tq, BRANCH_W), lambda i, j: (i, j, qoff)),
            kv, kv,
            pl.BlockSpec((1, MEM_HD), lambda i, j: (0, 0)),
        ],
        out_specs=pl.BlockSpec((1, tq, BRANCH_W), lambda i, j: (i, j, 0)),
        out_shape=jax.ShapeDtypeStruct((b, t, BRANCH_W), F32),
        compiler_params=_params("parallel", "parallel"),
        name="mem_attn",
    )(P, kn, mv, qg)


def _merge_body(x_ref, y_ref, bonus_ref, g_ref, yb_ref, ym_ref, g0_ref, g1_ref, g2_ref,
                lng_ref, lnb_ref, bd_ref, wbr_ref, wout_ref, o_ref):
    bd = bd_ref[...]
    y = y_ref[0]
    d = y - _dot_ones(y, bd) * (1.0 / HEAD_DIM)
    var = _dot_ones(d * d, bd) * (1.0 / HEAD_DIM)
    ya = (d * lax.rsqrt(var + LNX_EPS) * lng_ref[...] + lnb_ref[...] + bonus_ref[0]) * g_ref[0]
    m = (_sigmoid(g0_ref[0]) * _dot(ya.astype(BF16), wbr_ref[0])
         + _sigmoid(g1_ref[0]) * _dot(yb_ref[0].astype(BF16), wbr_ref[1])
         + _sigmoid(g2_ref[0]) * _dot(ym_ref[0].astype(BF16), wbr_ref[2]))
    o_ref[0] = x_ref[0] + _dot(m.astype(BF16), wout_ref[...])


def _merge(x, y, bonus, g, yb, ym, P, lng, lnb, bd, wbr, wout, tq=256):
    b, t, _ = x.shape
    tok = pl.BlockSpec((1, tq, BRANCH_W), lambda i, j: (i, j, 0))
    xblk = pl.BlockSpec((1, tq, D_MODEL), lambda i, j: (i, j, 0))
    goff = (RWKV_PAD + 2048) // D_MODEL

    def gate(n):
        return pl.BlockSpec((1, tq, D_MODEL), lambda i, j: (i, j, goff + n))

    def full(a):
        return pl.BlockSpec(a.shape, lambda i, j: (0,) * a.ndim)

    return pl.pallas_call(
        _merge_body,
        grid=(b, t // tq),
        in_specs=[xblk, tok, tok, tok, tok, tok, gate(0), gate(1), gate(2),
                  full(lng), full(lnb), full(bd), full(wbr), full(wout)],
        out_specs=xblk,
        out_shape=jax.ShapeDtypeStruct(x.shape, F32),
        compiler_params=_params("parallel", "parallel"),
        name="merge",
    )(x, y, bonus, g, yb, ym, P, P, P, lng, lnb, bd, wbr, wout)


FF_TILE = 1408


def _ffn_body(tq, x_ref, g_ref, wup_ref, cw_ref, cb_ref, wdn_ref, o_ref, carry):
    @pl.when(pl.program_id(1) == 0)
    def _():
        carry[...] = jnp.zeros_like(carry)

    x = x_ref[0]
    h = _rms(x, g_ref[...]).astype(BF16)
    row = lax.broadcasted_iota(jnp.int32, (tq, 1), 0)
    acc = jnp.zeros((tq, D_MODEL), F32)
    for f in range(D_FF // FF_TILE):
        halves = []
        for half in range(2):
            c0 = half * D_FF + f * FF_TILE
            sl = slice(c0, c0 + FF_TILE)
            hh = _dot(h, wup_ref[:, sl])
            cp = carry[:, sl]
            s1 = jnp.where(row == 0, cp[7:8], pltpu.roll(hh, 1, 0))
            s2 = jnp.where(row == 0, cp[6:7], jnp.where(row == 1, cp[7:8], pltpu.roll(hh, 2, 0)))
            carry[:, sl] = hh[tq - 8:, :]
            cw = cw_ref[:, sl]
            halves.append(cb_ref[:, sl] + s2 * cw[0:1] + s1 * cw[1:2] + hh * cw[2:3])
        gate, val = halves
        u = gate * _sigmoid(gate) * val
        acc = acc + _dot(u.astype(BF16), wdn_ref[f * FF_TILE:(f + 1) * FF_TILE, :])
    o_ref[0] = x + acc


def _ffn(x, g, wup, cw, cb, wdn, tq=256):
    b, t, _ = x.shape
    xblk = pl.BlockSpec((1, tq, D_MODEL), lambda i, j: (i, j, 0))

    def full(a):
        return pl.BlockSpec(a.shape, lambda i, j: (0,) * a.ndim)

    return pl.pallas_call(
        functools.partial(_ffn_body, tq),
        grid=(b, t // tq),
        in_specs=[xblk, full(g), full(wup), full(cw), full(cb), full(wdn)],
        out_specs=xblk,
        out_shape=jax.ShapeDtypeStruct(x.shape, F32),
        scratch_shapes=[pltpu.VMEM((8, 2 * D_FF), F32)],
        compiler_params=_params("parallel", "arbitrary"),
        name="conv_ffn",
    )(x, g, wup, cw, cb, wdn)


def _row(v):
    return v.reshape(1, -1)


def kernel(x, mem, norm1_g, w_in, shift_mu, decay_w0, decay_w2, iclr_a0, iclr_a2, gate_g2, k_k, k_a, r_k, lnx_g, lnx_b, vres_v0, vres_v1, vres_v2, sb_q_norm_g, sb_k_norm_g, mem_norm_g, w_mem_kv, mem_q_norm_g, mem_k_norm_g, w_branch, w_out, norm2_g, w_up, conv_w, conv_b, w_down):
    b, t, _ = x.shape
    assert 2 * b * N_HEADS == LANES, "recurrence layout packs (value-half, batch, head) onto the lanes"
    depth = w_in.shape[0]

    head_id = jnp.arange(BRANCH_W) // HEAD_DIM
    bd = (head_id[:, None] == head_id[None, :]).astype(BF16)
    tri = jnp.arange(LANES)
    u_ext = jnp.concatenate([(tri[:, None] > tri[None, :]).astype(BF16),
                             jnp.ones((LANES, LANES), BF16)], axis=1)
    u_ext = jnp.concatenate([u_ext, u_ext], axis=0)
    zpad64 = jnp.zeros((64, BRANCH_W), F32)

    v_first = None
    for l in range(depth):
        w_perm = jnp.concatenate(
            [w_in[l][:, :RWKV_COLS], jnp.zeros((D_MODEL, RWKV_PAD - RWKV_COLS), F32), w_in[l][:, RWKV_COLS:]],
            axis=1).astype(BF16)
        P = _proj(x.reshape(b * t, D_MODEL), _row(norm1_g[l]), w_perm).reshape(b, t, P_COLS)

        vecs = (_row(shift_mu[l]), _row(decay_w0[l]), _row(iclr_a0[l]), _row(k_k[l]), _row(k_a[l]), _row(r_k[l]))
        mats = (jnp.concatenate([decay_w2[l], zpad64], axis=0).astype(BF16),
                jnp.concatenate([zpad64, iclr_a2[l]], axis=0).astype(BF16),
                gate_g2[l].astype(BF16))
        vres = None
        if l > 0:
            v1 = jnp.pad(vres_v1[l - 1], ((0, 0), (0, LANES - vres_v1.shape[-1]))).astype(BF16)
            v2 = jnp.pad(vres_v2[l - 1], ((0, LANES - vres_v2.shape[-2]), (0, 0))).astype(BF16)
            vres = (v_first, _row(vres_v0[l - 1]), v1, v2)
        r, dec, k2, v, nkk, bb, g, bonus = _prep(P, vecs, mats, bd, vres)
        if l == 0:
            v_first = v
        y = _recurrence(_to_key_lanes(dec), _to_key_lanes(nkk), _to_key_lanes(bb), _to_key_lanes(k2),
                        _to_key_lanes(r), _to_val_lanes(v))
        y = _from_val_lanes(y, b)

        qg2 = _row(jnp.concatenate([sb_q_norm_g[l], sb_q_norm_g[l]]))
        kg2 = _row(jnp.concatenate([sb_k_norm_g[l], sb_k_norm_g[l]]))
        yb = _sb_attention(P, qg2, kg2, u_ext)

        kn, mv = _mem_kv(mem, _row(mem_norm_g[l]), w_mem_kv[l].astype(BF16), _row(mem_k_norm_g[l]))
        ym = _mem_attention(P, kn, mv, _row(mem_q_norm_g[l]))

        x = _merge(x, y, bonus, g, yb, ym, P, _row(lnx_g[l]), _row(lnx_b[l]), bd,
                   w_branch[l].astype(BF16), w_out[l].astype(BF16))
        x = _ffn(x, _row(norm2_g[l]), w_up[l].astype(BF16), conv_w[l], _row(conv_b[l]), w_down[l].astype(BF16))
    return x
```

```python
import functools

import jax
import jax.numpy as jnp
from jax import lax
from jax.experimental import pallas as pl
from jax.experimental.pallas import tpu as pltpu

F32 = jnp.float32
BF16 = jnp.bfloat16

D_MODEL = 1024
N_HEADS = 8
HEAD_DIM = 64
BRANCH_W = 512
RWKV_COLS = 1792
RWKV_PAD = 2048
REST_COLS = 5120
MEM_Q_OFF = 1536
GATE_OFF = 2048
LOG2E = 1.4426950408889634
MEM_LEN = 256
MEM_HEADS = 4
MEM_HD = 128
D_FF = 2816
NORM_EPS = 1e-6
LNX_EPS = 64e-5
LANES = 128
MASKED_LOGIT = -1e30
SB_ROWS = 128
VMEM_LIMIT = 56 * 1024 * 1024

NT_DIMS = (((1,), (1,)), ((), ()))


def _rms(xf, g, eps=NORM_EPS):
    return xf * lax.rsqrt(jnp.mean(xf * xf, axis=-1, keepdims=True) + eps) * g


def _sigmoid(x):
    return 1.0 / (1.0 + jnp.exp(-x))


def _softplus(x):
    return jnp.maximum(x, 0.0) + jnp.log(1.0 + jnp.exp(-jnp.abs(x)))


def _dot(a, b):
    return jnp.dot(a, b, preferred_element_type=F32)


def _split3(x):
    h1 = x.astype(BF16)
    r1 = x - h1.astype(F32)
    h2 = r1.astype(BF16)
    h3 = (r1 - h2.astype(F32)).astype(BF16)
    return h1, h2, h3


def _dot_ones(x, ones_bf16):
    h1, h2, h3 = _split3(x)
    return _dot(h1, ones_bf16) + _dot(h2, ones_bf16) + _dot(h3, ones_bf16)


def _params(*sem):
    return pltpu.CompilerParams(dimension_semantics=sem, vmem_limit_bytes=VMEM_LIMIT)


def _proj_body(x_ref, g_ref, w_ref, o_ref, h_scr):
    @pl.when(pl.program_id(1) == 0)
    def _():
        h_scr[...] = _rms(x_ref[...], g_ref[...]).astype(BF16)

    o_ref[...] = _dot(h_scr[...], w_ref[...]).astype(o_ref.dtype)


def _proj(x2d, g, w_bf16, out_dtype, tq=1024, tc=1024):
    n = x2d.shape[0]
    cols = w_bf16.shape[1]
    return pl.pallas_call(
        _proj_body,
        grid=(n // tq, cols // tc),
        in_specs=[
            pl.BlockSpec((tq, D_MODEL), lambda i, j: (i, 0)),
            pl.BlockSpec((1, D_MODEL), lambda i, j: (0, 0)),
            pl.BlockSpec((D_MODEL, tc), lambda i, j: (0, j)),
        ],
        out_specs=pl.BlockSpec((tq, tc), lambda i, j: (i, j)),
        out_shape=jax.ShapeDtypeStruct((n, cols), out_dtype),
        scratch_shapes=[pltpu.VMEM((tq, D_MODEL), BF16)],
        compiler_params=_params("parallel", "arbitrary"),
        cost_estimate=pl.CostEstimate(flops=2 * n * D_MODEL * cols, transcendentals=n,
                                      bytes_accessed=4 * n * D_MODEL + 2 * D_MODEL * cols * (n // tq)
                                      + n * cols * jnp.dtype(out_dtype).itemsize),
        name="proj",
    )(x2d, g, w_bf16)


def _prep_body(has_vres, tq, *refs):
    if has_vres:
        (p_ref, mu_ref, w0_ref, w2_ref, a0_ref, a2_ref, g2_ref, kk_ref, ka_ref, rk_ref, bd_ref,
         vf_ref, v0_ref, v1_ref, v2_ref,
         r_o, dec_o, k_o, v_o, nkk_o, bb_o, g_o, bonus_o, carry) = refs
    else:
        (p_ref, mu_ref, w0_ref, w2_ref, a0_ref, a2_ref, g2_ref, kk_ref, ka_ref, rk_ref, bd_ref,
         r_o, dec_o, k_o, v_o, nkk_o, bb_o, g_o, bonus_o, carry) = refs

    @pl.when(pl.program_id(1) == 0)
    def _():
        carry[...] = jnp.zeros_like(carry)

    p = p_ref[0][:, :RWKV_COLS]
    row = lax.broadcasted_iota(jnp.int32, (tq, 1), 0)
    prev = jnp.where(row == 0, carry[7:8, :], pltpu.roll(p, 1, 0))
    carry[...] = p[tq - 8:, :]
    ps = p + (prev - p) * mu_ref[...]

    r = ps[:, 0:512]
    k = ps[:, 512:1024]
    v = ps[:, 1024:1536]
    wa = ps[:, 1536:1664]
    gl = ps[:, 1664:1792]

    wlog = -_softplus(-(w0_ref[...] + _dot(jnp.tanh(wa).astype(BF16), w2_ref[...]))) - 0.5
    dec = jnp.exp(-jnp.exp(wlog))
    a = _sigmoid(a0_ref[...] + _dot(wa.astype(BF16), a2_ref[...]))
    g = _dot(_sigmoid(gl).astype(BF16), g2_ref[...])
    if has_vres:
        lo = _dot(v.astype(BF16), v1_ref[...])
        v = v + (vf_ref[0] - v) * _sigmoid(v0_ref[...] + _dot(lo.astype(BF16), v2_ref[...]))
    bd = bd_ref[...]
    kk = k * kk_ref[...]
    kk = kk / jnp.maximum(jnp.sqrt(_dot_ones(kk * kk, bd)), 1e-12)
    k2 = k * (1.0 + (a - 1.0) * ka_ref[...])
    bonus = _dot_ones(r * k2 * rk_ref[...], bd) * v

    r_o[0] = r
    dec_o[0] = dec
    k_o[0] = k2
    v_o[0] = v
    nkk_o[0] = -kk
    bb_o[0] = kk * a
    g_o[0] = g
    bonus_o[0] = bonus


def _prep(P, vecs, mats, bd, vres, tq=256):
    b, t, _ = P.shape
    has_vres = vres is not None
    tok = pl.BlockSpec((1, tq, BRANCH_W), lambda i, j: (i, j, 0))

    def full(a):
        return pl.BlockSpec(a.shape, lambda i, j: (0,) * a.ndim)

    mu, w0, a0, kk, ka, rk = vecs
    w2, a2, g2 = mats
    args = [P, mu, w0, w2, a0, a2, g2, kk, ka, rk, bd]
    in_specs = [pl.BlockSpec((1, tq, RWKV_PAD), lambda i, j: (i, j, 0))] + [full(a) for a in args[1:]]
    if has_vres:
        vf, v0, v1, v2 = vres
        args += [vf, v0, v1, v2]
        in_specs += [tok, full(v0), full(v1), full(v2)]
    out = jax.ShapeDtypeStruct((b, t, BRANCH_W), F32)
    return pl.pallas_call(
        functools.partial(_prep_body, has_vres, tq),
        grid=(b, t // tq),
        in_specs=in_specs,
        out_specs=[tok] * 8,
        out_shape=[out] * 8,
        scratch_shapes=[pltpu.VMEM((8, RWKV_COLS), F32)],
        compiler_params=_params("parallel", "arbitrary"),
        name="rwkv_prep",
    )(*args)


HALF_J = HEAD_DIM // 2
SUBLANES = 8


def _rec_body(tc, w_ref, nk_ref, b_ref, k_ref, r_ref, v_ref, after_ref, y_ref, s_scr):
    del after_ref

    @pl.when(pl.program_id(0) == 0)
    def _():
        s_scr[...] = jnp.zeros_like(s_scr)

    groups = [pl.ds(g * SUBLANES, SUBLANES) for g in range(HEAD_DIM // SUBLANES)]

    def fold(x):
        return x + pltpu.roll(x, HEAD_DIM, 1)

    def step(t, carry):
        part = [None] * len(groups)
        for j in range(HALF_J):
            nk = nk_ref[t, pl.ds(j, 1), :]
            for g, rows in enumerate(groups):
                term = s_scr[j, rows, :] * nk
                part[g] = term if part[g] is None else part[g] + term
        sa = [fold(p) for p in part]
        v = [v_ref[t, rows, :] for rows in groups]
        yacc = [None] * len(groups)
        for j in range(HALF_J):
            w = w_ref[t, pl.ds(j, 1), :]
            bb = b_ref[t, pl.ds(j, 1), :]
            kk = k_ref[t, pl.ds(j, 1), :]
            rr = r_ref[t, pl.ds(j, 1), :]
            for g, rows in enumerate(groups):
                sn = s_scr[j, rows, :] * w + (sa[g] * bb + v[g] * kk)
                s_scr[j, rows, :] = sn
                term = sn * rr
                yacc[g] = term if yacc[g] is None else yacc[g] + term
        for g, rows in enumerate(groups):
            y_ref[t, rows, :] = yacc[g]
        return carry

    lax.fori_loop(0, tc, step, 0)
    y = y_ref[...].reshape(tc * HEAD_DIM, LANES)
    y_ref[...] = fold(y).reshape(tc, HEAD_DIM, LANES)


def _recurrence(dec, nkk, bb, k2, r, v, after, tc=32):
    t = dec.shape[0]
    jspec = pl.BlockSpec((tc, HALF_J, LANES), lambda i: (i, 0, 0))
    ispec = pl.BlockSpec((tc, HEAD_DIM, LANES), lambda i: (i, 0, 0))
    return pl.pallas_call(
        functools.partial(_rec_body, tc),
        grid=(t // tc,),
        in_specs=[jspec] * 5 + [ispec, pl.BlockSpec(memory_space=pl.ANY)],
        out_specs=ispec,
        out_shape=jax.ShapeDtypeStruct((t, HEAD_DIM, LANES), F32),
        scratch_shapes=[pltpu.VMEM((HALF_J, HEAD_DIM, LANES), F32)],
        compiler_params=_params("arbitrary"),
        cost_estimate=pl.CostEstimate(flops=9 * t * HALF_J * HEAD_DIM * LANES, transcendentals=0,
                                      bytes_accessed=4 * t * LANES * (5 * HALF_J + 2 * HEAD_DIM)),
        name="rwkv_rec",
    )(dec, nkk, bb, k2, r, v, after)


def _to_val_lanes(a):
    b, t, _ = a.shape
    a = a.reshape(b, t, N_HEADS, HEAD_DIM).transpose(1, 3, 0, 2).reshape(t, HEAD_DIM, b * N_HEADS)
    return jnp.concatenate([a, a], axis=-1)


def _to_key_lanes(a):
    b, t, _ = a.shape
    a = a.reshape(b, t, N_HEADS, 2, HALF_J).transpose(1, 4, 3, 0, 2)
    return a.reshape(t, HALF_J, 2 * b * N_HEADS)


def _from_val_lanes(y, b):
    t = y.shape[0]
    y = y[:, :, :b * N_HEADS].reshape(t, HEAD_DIM, b, N_HEADS).transpose(2, 0, 3, 1)
    return y.reshape(b, t, BRANCH_W)


def _sb_body(tq, tk, q_ref, k_ref, v_ref, qg_ref, kg_ref, u_ref, o_ref,
             kn_scr, vb_scr, qn_scr, hl_scr, lsig_scr, logit_scr, run_scr, acc_scr):
    qi = pl.program_id(2)
    nsub = tq // tk
    nblk = kn_scr.shape[0]
    lane = lax.broadcasted_iota(jnp.int32, (1, LANES), 1)
    first = lane < HEAD_DIM
    m_a = first.astype(F32)
    m_b = 1.0 - m_a

    def headnorm(x, g):
        x2 = x * x
        s_a = jnp.sum(x2 * m_a, axis=-1, keepdims=True)
        s_b = jnp.sum(x2 * m_b, axis=-1, keepdims=True)
        ms = jnp.where(first, s_a, s_b) * (1.0 / HEAD_DIM)
        return x * lax.rsqrt(ms + NORM_EPS) * g

    @pl.when(qi == 0)
    def _():
        kn = headnorm(k_ref[0].astype(F32), kg_ref[...])
        v = v_ref[0].astype(F32)
        kn_scr[:, :tk, :] = (kn * m_a).astype(BF16).reshape(nblk, tk, LANES)
        kn_scr[:, tk:, :] = (kn * m_b).astype(BF16).reshape(nblk, tk, LANES)
        vb_scr[:, :tk, :] = (v * m_a).astype(BF16).reshape(nblk, tk, LANES)
        vb_scr[:, tk:, :] = (v * m_b).astype(BF16).reshape(nblk, tk, LANES)

    qn_scr[...] = (headnorm(q_ref[0].astype(F32), qg_ref[...]) * (LOG2E * HEAD_DIM ** -0.5)).astype(BF16)
    run_scr[...] = jnp.zeros_like(run_scr)
    acc_scr[...] = jnp.zeros_like(acc_scr)
    uu = u_ref[...]
    rowi = lax.broadcasted_iota(jnp.int32, (SB_ROWS, 2 * tk), 0)
    coli = lax.broadcasted_iota(jnp.int32, (SB_ROWS, 2 * tk), 1) & (tk - 1)

    last = qi * nsub + (nsub - 1)
    heads = (slice(0, tk), slice(tk, 2 * tk))
    groups = [slice(s * SB_ROWS, (s + 1) * SB_ROWS) for s in range(tq // SB_ROWS)]

    def score(m, masked, rows):
        z = lax.dot_general(qn_scr[rows, :], kn_scr[jnp.maximum(last - m, 0)], NT_DIMS,
                            preferred_element_type=F32)
        l1m = -(jnp.maximum(z, 0.0) + jnp.log2(1.0 + jnp.exp2(-jnp.abs(z))))
        lsig = z + l1m
        if masked:
            mask = (coli + ((nsub - 1 - m) * tk - rows.start)) < rowi
            l1m = jnp.where(mask, l1m, 0.0)
            lsig = jnp.where(mask, lsig, MASKED_LOGIT)
        hi = l1m.astype(BF16)
        lo = (l1m - hi.astype(F32)).astype(BF16)
        for h, sl in enumerate(heads):
            hl_scr[h, rows, :] = jnp.concatenate([hi[:, sl], lo[:, sl]], axis=1)
        lsig_scr[rows, :] = lsig

    def suffix(rows):
        for h, sl in enumerate(heads):
            ce = _dot(hl_scr[h, rows, :], uu)
            run = run_scr[h, rows, :]
            logit_scr[rows, sl] = lsig_scr[rows, sl] + ce[:, :tk] + run
            run_scr[h, rows, :] = run + ce[:, tk:]

    def weigh(m, rows):
        acc_scr[rows, :] += _dot(jnp.exp2(logit_scr[rows, :]).astype(BF16), vb_scr[last - m])

    def body(m, carry):
        for rows in groups:
            weigh(m, rows)
            suffix(rows)
            score(m + 2, False, rows)
        return carry

    def live(rows, m):
        return (nsub - 1 - m) * tk <= rows.stop - 2

    def cut(rows, m):
        return (nsub - m) * tk - 1 >= rows.start

    for rows in groups:
        if live(rows, 0):
            score(0, cut(rows, 0), rows)
    for rows in groups:
        if live(rows, 0):
            suffix(rows)
        if live(rows, 1):
            score(1, cut(rows, 1), rows)
    for m in range(nsub):
        for rows in groups:
            if live(rows, m):
                weigh(m, rows)
            if live(rows, m + 1):
                suffix(rows)
            if live(rows, m + 2):
                score(m + 2, cut(rows, m + 2), rows)
    lax.fori_loop(nsub, (qi + 1) * nsub, body, 0)
    o_ref[0] = acc_scr[...]


def _sb_attention(P, qg, kg, uu, tq=1024, tk=LANES):
    b, t, _ = P.shape
    tq = min(tq, t)
    npair = N_HEADS // 2
    qoff, koff, voff = 0, BRANCH_W // LANES, 2 * BRANCH_W // LANES
    stacked = pltpu.VMEM((t // tk, 2 * tk, LANES), BF16)
    return pl.pallas_call(
        functools.partial(_sb_body, tq, tk),
        grid=(b, npair, t // tq),
        in_specs=[
            pl.BlockSpec((1, tq, LANES), lambda i, h, j: (i, j, qoff + h)),
            pl.BlockSpec((1, t, LANES), lambda i, h, j: (i, 0, koff + h)),
            pl.BlockSpec((1, t, LANES), lambda i, h, j: (i, 0, voff + h)),
            pl.BlockSpec((1, LANES), lambda i, h, j: (0, 0)),
            pl.BlockSpec((1, LANES), lambda i, h, j: (0, 0)),
            pl.BlockSpec((2 * tk, 2 * tk), lambda i, h, j: (0, 0)),
        ],
        out_specs=pl.BlockSpec((1, tq, LANES), lambda i, h, j: (i, j, h)),
        out_shape=jax.ShapeDtypeStruct((b, t, BRANCH_W), F32),
        scratch_shapes=[stacked, stacked,
                        pltpu.VMEM((tq, LANES), BF16),
                        pltpu.VMEM((2, tq, 2 * tk), BF16),
                        pltpu.VMEM((tq, 2 * tk), F32),
                        pltpu.VMEM((tq, 2 * tk), F32),
                        pltpu.VMEM((2, tq, LANES), F32),
                        pltpu.VMEM((tq, LANES), F32)],
        compiler_params=_params("parallel", "parallel", "arbitrary"),
        cost_estimate=pl.CostEstimate(flops=b * N_HEADS * t * t * (2 * HEAD_DIM + 4 * tk + 2 * HEAD_DIM) // 2,
                                      transcendentals=3 * b * N_HEADS * t * t // 2,
                                      bytes_accessed=2 * 3 * b * t * BRANCH_W + 4 * b * t * BRANCH_W),
        name="sb_attn",
    )(P, P, P, qg, kg, uu)


def _memkv_body(m_ref, g_ref, w_ref, kg_ref, k_o, v_o):
    kv = _dot(_rms(m_ref[0], g_ref[...]).astype(BF16), w_ref[...])
    for h in range(MEM_HEADS):
        sl = slice(h * MEM_HD, (h + 1) * MEM_HD)
        k_o[0, :, sl] = _rms(kv[:, sl], kg_ref[...]).astype(BF16)
    v_o[0] = kv[:, BRANCH_W:].astype(BF16)


def _mem_kv(mem, g, w_bf16, kg):
    b = mem.shape[0]
    out = jax.ShapeDtypeStruct((b, MEM_LEN, BRANCH_W), BF16)
    blk = pl.BlockSpec((1, MEM_LEN, BRANCH_W), lambda i: (i, 0, 0))
    return pl.pallas_call(
        _memkv_body,
        grid=(b,),
        in_specs=[
            pl.BlockSpec((1, MEM_LEN, D_MODEL), lambda i: (i, 0, 0)),
            pl.BlockSpec((1, D_MODEL), lambda i: (0, 0)),
            pl.BlockSpec((D_MODEL, 2 * BRANCH_W), lambda i: (0, 0)),
            pl.BlockSpec((1, MEM_HD), lambda i: (0, 0)),
        ],
        out_specs=[blk, blk],
        out_shape=[out, out],
        compiler_params=_params("parallel"),
        name="mem_kv",
    )(mem, g, w_bf16, kg)


def _memattn_body(q_ref, k_ref, v_ref, qg_ref, o_ref):
    q = q_ref[0].astype(F32)
    for h in range(MEM_HEADS):
        sl = slice(h * MEM_HD, (h + 1) * MEM_HD)
        qh = (_rms(q[:, sl], qg_ref[...]) * (MEM_HD ** -0.5)).astype(BF16)
        s = lax.dot_general(qh, k_ref[0, :, sl], NT_DIMS, preferred_element_type=F32)
        p = jnp.exp(s - jnp.max(s, axis=-1, keepdims=True))
        den = jnp.sum(p, axis=-1, keepdims=True)
        o_ref[0, :, sl] = _dot(p.astype(BF16), v_ref[0, :, sl]) / den


def _mem_attention(P, kn, mv, qg, tq=512):
    b, t, _ = P.shape
    qoff = MEM_Q_OFF // BRANCH_W
    kv = pl.BlockSpec((1, MEM_LEN, BRANCH_W), lambda i, j: (i, 0, 0))
    return pl.pallas_call(
        _memattn_body,
        grid=(b, t // tq),
        in_specs=[
            pl.BlockSpec((1, tq, BRANCH_W), lambda i, j: (i, j, qoff)),
            kv, kv,
            pl.BlockSpec((1, MEM_HD), lambda i, j: (0, 0)),
        ],
        out_specs=pl.BlockSpec((1, tq, BRANCH_W), lambda i, j: (i, j, 0)),
        out_shape=jax.ShapeDtypeStruct((b, t, BRANCH_W), F32),
        compiler_params=_params("parallel", "parallel"),
        name="mem_attn",
    )(P, kn, mv, qg)


def _merge_body(x_ref, y_ref, bonus_ref, g_ref, yb_ref, ym_ref, g0_ref, g1_ref, g2_ref,
                lng_ref, lnb_ref, bd_ref, wbr_ref, wout_ref, o_ref):
    bd = bd_ref[...]
    y = y_ref[0]
    d = y - _dot_ones(y, bd) * (1.0 / HEAD_DIM)
    var = _dot_ones(d * d, bd) * (1.0 / HEAD_DIM)
    ya = (d * lax.rsqrt(var + LNX_EPS) * lng_ref[...] + lnb_ref[...] + bonus_ref[0]) * g_ref[0]
    m = (_sigmoid(g0_ref[0].astype(F32)) * _dot(ya.astype(BF16), wbr_ref[0])
         + _sigmoid(g1_ref[0].astype(F32)) * _dot(yb_ref[0].astype(BF16), wbr_ref[1])
         + _sigmoid(g2_ref[0].astype(F32)) * _dot(ym_ref[0].astype(BF16), wbr_ref[2]))
    o_ref[0] = x_ref[0] + _dot(m.astype(BF16), wout_ref[...])


def _merge(x, y, bonus, g, yb, ym, gates, lng, lnb, bd, wbr, wout, tq=256):
    b, t, _ = x.shape
    tok = pl.BlockSpec((1, tq, BRANCH_W), lambda i, j: (i, j, 0))
    xblk = pl.BlockSpec((1, tq, D_MODEL), lambda i, j: (i, j, 0))
    goff = GATE_OFF // D_MODEL

    def gate(n):
        return pl.BlockSpec((1, tq, D_MODEL), lambda i, j: (i, j, goff + n))

    def full(a):
        return pl.BlockSpec(a.shape, lambda i, j: (0,) * a.ndim)

    return pl.pallas_call(
        _merge_body,
        grid=(b, t // tq),
        in_specs=[xblk, tok, tok, tok, tok, tok, gate(0), gate(1), gate(2),
                  full(lng), full(lnb), full(bd), full(wbr), full(wout)],
        out_specs=xblk,
        out_shape=jax.ShapeDtypeStruct(x.shape, F32),
        compiler_params=_params("parallel", "parallel"),
        name="merge",
    )(x, y, bonus, g, yb, ym, gates, gates, gates, lng, lnb, bd, wbr, wout)


FF_TILE = 1408


def _ffn_body(tq, x_ref, g_ref, wup_ref, cw_ref, cb_ref, wdn_ref, o_ref, carry):
    @pl.when(pl.program_id(1) == 0)
    def _():
        carry[...] = jnp.zeros_like(carry)

    x = x_ref[0]
    h = _rms(x, g_ref[...]).astype(BF16)
    row = lax.broadcasted_iota(jnp.int32, (tq, 1), 0)
    acc = jnp.zeros((tq, D_MODEL), F32)
    for f in range(D_FF // FF_TILE):
        halves = []
        for half in range(2):
            c0 = half * D_FF + f * FF_TILE
            sl = slice(c0, c0 + FF_TILE)
            hh = _dot(h, wup_ref[:, sl])
            cp = carry[:, sl]
            s1 = jnp.where(row == 0, cp[7:8], pltpu.roll(hh, 1, 0))
            s2 = jnp.where(row == 0, cp[6:7], jnp.where(row == 1, cp[7:8], pltpu.roll(hh, 2, 0)))
            carry[:, sl] = hh[tq - 8:, :]
            cw = cw_ref[:, sl]
            halves.append(cb_ref[:, sl] + s2 * cw[0:1] + s1 * cw[1:2] + hh * cw[2:3])
        gate, val = halves
        u = gate * _sigmoid(gate) * val
        acc = acc + _dot(u.astype(BF16), wdn_ref[f * FF_TILE:(f + 1) * FF_TILE, :])
    o_ref[0] = x + acc


def _ffn(x, g, wup, cw, cb, wdn, tq=256):
    b, t, _ = x.shape
    xblk = pl.BlockSpec((1, tq, D_MODEL), lambda i, j: (i, j, 0))

    def full(a):
        return pl.BlockSpec(a.shape, lambda i, j: (0,) * a.ndim)

    return pl.pallas_call(
        functools.partial(_ffn_body, tq),
        grid=(b, t // tq),
        in_specs=[xblk, full(g), full(wup), full(cw), full(cb), full(wdn)],
        out_specs=xblk,
        out_shape=jax.ShapeDtypeStruct(x.shape, F32),
        scratch_shapes=[pltpu.VMEM((8, 2 * D_FF), F32)],
        compiler_params=_params("parallel", "arbitrary"),
        name="conv_ffn",
    )(x, g, wup, cw, cb, wdn)


def _row(v):
    return v.reshape(1, -1)


def kernel(x, mem, norm1_g, w_in, shift_mu, decay_w0, decay_w2, iclr_a0, iclr_a2, gate_g2, k_k, k_a, r_k, lnx_g, lnx_b, vres_v0, vres_v1, vres_v2, sb_q_norm_g, sb_k_norm_g, mem_norm_g, w_mem_kv, mem_q_norm_g, mem_k_norm_g, w_branch, w_out, norm2_g, w_up, conv_w, conv_b, w_down):
    b, t, _ = x.shape
    assert 2 * b * N_HEADS == LANES, "recurrence layout packs (key-half, batch, head) onto the lanes"
    depth = w_in.shape[0]

    head_id = jnp.arange(BRANCH_W) // HEAD_DIM
    bd = (head_id[:, None] == head_id[None, :]).astype(BF16)
    tri = jnp.arange(LANES)
    u_ext = jnp.concatenate([(tri[:, None] > tri[None, :]).astype(BF16),
                             jnp.ones((LANES, LANES), BF16)], axis=1)
    u_ext = jnp.concatenate([u_ext, u_ext], axis=0)
    zpad64 = jnp.zeros((64, BRANCH_W), F32)

    v_first = None
    for l in range(depth):
        x2d = x.reshape(b * t, D_MODEL)
        w_rwkv = jnp.pad(w_in[l][:, :RWKV_COLS], ((0, 0), (0, RWKV_PAD - RWKV_COLS))).astype(BF16)
        P = _proj(x2d, _row(norm1_g[l]), w_rwkv, F32).reshape(b, t, RWKV_PAD)

        vecs = (_row(shift_mu[l]), _row(decay_w0[l]), _row(iclr_a0[l]), _row(k_k[l]), _row(k_a[l]), _row(r_k[l]))
        mats = (jnp.concatenate([decay_w2[l], zpad64], axis=0).astype(BF16),
                jnp.concatenate([zpad64, iclr_a2[l]], axis=0).astype(BF16),
                gate_g2[l].astype(BF16))
        vres = None
        if l > 0:
            v1 = jnp.pad(vres_v1[l - 1], ((0, 0), (0, LANES - vres_v1.shape[-1]))).astype(BF16)
            v2 = jnp.pad(vres_v2[l - 1], ((0, LANES - vres_v2.shape[-2]), (0, 0))).astype(BF16)
            vres = (v_first, _row(vres_v0[l - 1]), v1, v2)
        r, dec, k2, v, nkk, bb, g, bonus = _prep(P, vecs, mats, bd, vres)
        if l == 0:
            v_first = v

        Pr = _proj(x2d, _row(norm1_g[l]), w_in[l][:, RWKV_COLS:].astype(BF16), BF16).reshape(b, t, REST_COLS)
        kn, mv = _mem_kv(mem, _row(mem_norm_g[l]), w_mem_kv[l].astype(BF16), _row(mem_k_norm_g[l]))
        ym = _mem_attention(Pr, kn, mv, _row(mem_q_norm_g[l]))

        y = _recurrence(_to_key_lanes(dec), _to_key_lanes(nkk), _to_key_lanes(bb), _to_key_lanes(k2),
                        _to_key_lanes(r), _to_val_lanes(v), ym)
        y = _from_val_lanes(y, b)

        qg2 = _row(jnp.concatenate([sb_q_norm_g[l], sb_q_norm_g[l]]))
        kg2 = _row(jnp.concatenate([sb_k_norm_g[l], sb_k_norm_g[l]]))
        yb = _sb_attention(Pr, qg2, kg2, u_ext)

        x = _merge(x, y, bonus, g, yb, ym, Pr, _row(lnx_g[l]), _row(lnx_b[l]), bd,
                   w_branch[l].astype(BF16), w_out[l].astype(BF16))
        x = _ffn(x, _row(norm2_g[l]), w_up[l].astype(BF16), conv_w[l], _row(conv_b[l]), w_down[l].astype(BF16))
    return x
```

```python
import functools

import jax
import jax.numpy as jnp
from jax import lax
from jax.experimental import pallas as pl
from jax.experimental.pallas import tpu as pltpu

F32 = jnp.float32
BF16 = jnp.bfloat16

D_MODEL = 1024
N_HEADS = 8
HEAD_DIM = 64
BRANCH_W = 512
RWKV_COLS = 1792
RWKV_PAD = 2048
REST_COLS = 5120
MEM_Q_OFF = 1536
GATE_OFF = 2048
LOG2E = 1.4426950408889634
MEM_LEN = 256
MEM_HEADS = 4
MEM_HD = 128
D_FF = 2816
NORM_EPS = 1e-6
LNX_EPS = 64e-5
LANES = 128
MASKED_LOGIT = -1e30
SB_ROWS = 128
VMEM_LIMIT = 56 * 1024 * 1024

NT_DIMS = (((1,), (1,)), ((), ()))


def _rms(xf, g, eps=NORM_EPS):
    return xf * lax.rsqrt(jnp.mean(xf * xf, axis=-1, keepdims=True) + eps) * g


def _sigmoid(x):
    return 1.0 / (1.0 + jnp.exp(-x))


def _softplus(x):
    return jnp.maximum(x, 0.0) + jnp.log(1.0 + jnp.exp(-jnp.abs(x)))


def _dot(a, b):
    return jnp.dot(a, b, preferred_element_type=F32)


def _split3(x):
    h1 = x.astype(BF16)
    r1 = x - h1.astype(F32)
    h2 = r1.astype(BF16)
    h3 = (r1 - h2.astype(F32)).astype(BF16)
    return h1, h2, h3


def _dot_ones(x, ones_bf16):
    h1, h2, h3 = _split3(x)
    return _dot(h1, ones_bf16) + _dot(h2, ones_bf16) + _dot(h3, ones_bf16)


def _params(*sem):
    return pltpu.CompilerParams(dimension_semantics=sem, vmem_limit_bytes=VMEM_LIMIT)


def _proj_body(x_ref, g_ref, wa_ref, wb_ref, oa_ref, ob_ref):
    h = _rms(x_ref[...], g_ref[...]).astype(BF16)
    oa_ref[...] = _dot(h, wa_ref[...])
    ob_ref[...] = _dot(h, wb_ref[...]).astype(BF16)


def _proj(x2d, g, w_rwkv, w_rest, tq=256):
    n = x2d.shape[0]
    ca, cb = w_rwkv.shape[1], w_rest.shape[1]

    def full(a):
        return pl.BlockSpec(a.shape, lambda i: (0, 0))

    return pl.pallas_call(
        _proj_body,
        grid=(n // tq,),
        in_specs=[pl.BlockSpec((tq, D_MODEL), lambda i: (i, 0)), full(g), full(w_rwkv), full(w_rest)],
        out_specs=[pl.BlockSpec((tq, ca), lambda i: (i, 0)), pl.BlockSpec((tq, cb), lambda i: (i, 0))],
        out_shape=[jax.ShapeDtypeStruct((n, ca), F32), jax.ShapeDtypeStruct((n, cb), BF16)],
        compiler_params=_params("parallel"),
        cost_estimate=pl.CostEstimate(flops=2 * n * D_MODEL * (ca + cb), transcendentals=n,
                                      bytes_accessed=4 * n * D_MODEL + 2 * D_MODEL * (ca + cb) + n * (4 * ca + 2 * cb)),
        name="proj",
    )(x2d, g, w_rwkv, w_rest)


def _prep_body(has_vres, tq, *refs):
    if has_vres:
        (p_ref, mu_ref, w0_ref, w2_ref, a0_ref, a2_ref, g2_ref, kk_ref, ka_ref, rk_ref, bd_ref,
         vf_ref, v0_ref, v1_ref, v2_ref,
         r_o, dec_o, k_o, v_o, nkk_o, bb_o, g_o, bonus_o, carry) = refs
        vfirst_o = None
    else:
        (p_ref, mu_ref, w0_ref, w2_ref, a0_ref, a2_ref, g2_ref, kk_ref, ka_ref, rk_ref, bd_ref,
         r_o, dec_o, k_o, v_o, nkk_o, bb_o, g_o, bonus_o, vfirst_o, carry) = refs

    @pl.when(pl.program_id(1) == 0)
    def _():
        carry[...] = jnp.zeros_like(carry)

    p = p_ref[0][:, :RWKV_COLS]
    row = lax.broadcasted_iota(jnp.int32, (tq, 1), 0)
    prev = jnp.where(row == 0, carry[7:8, :], pltpu.roll(p, 1, 0))
    carry[...] = p[tq - 8:, :]
    ps = p + (prev - p) * mu_ref[...]

    r = ps[:, 0:512]
    k = ps[:, 512:1024]
    v = ps[:, 1024:1536]
    wa = ps[:, 1536:1664]
    gl = ps[:, 1664:1792]

    wlog = -_softplus(-(w0_ref[...] + _dot(jnp.tanh(wa).astype(BF16), w2_ref[...]))) - 0.5
    dec = jnp.exp(-jnp.exp(wlog))
    a = _sigmoid(a0_ref[...] + _dot(wa.astype(BF16), a2_ref[...]))
    g = _dot(_sigmoid(gl).astype(BF16), g2_ref[...])
    if has_vres:
        lo = _dot(v.astype(BF16), v1_ref[...])
        v = v + (vf_ref[0] - v) * _sigmoid(v0_ref[...] + _dot(lo.astype(BF16), v2_ref[...]))
    bd = bd_ref[...]
    kk = k * kk_ref[...]
    kk = kk / jnp.maximum(jnp.sqrt(_dot_ones(kk * kk, bd)), 1e-12)
    k2 = k * (1.0 + (a - 1.0) * ka_ref[...])
    bonus = _dot_ones(r * k2 * rk_ref[...], bd) * v

    r_o[0] = r.T
    dec_o[0] = dec.T
    k_o[0] = k2.T
    v_o[0] = v.T
    nkk_o[0] = (-kk).T
    bb_o[0] = (kk * a).T
    g_o[0] = g
    bonus_o[0] = bonus
    if vfirst_o is not None:
        vfirst_o[0] = v


def _prep(P, vecs, mats, bd, vres, tq=256):
    b, t, _ = P.shape
    has_vres = vres is not None
    tok = pl.BlockSpec((1, tq, BRANCH_W), lambda i, j: (i, j, 0))

    def full(a):
        return pl.BlockSpec(a.shape, lambda i, j: (0,) * a.ndim)

    mu, w0, a0, kk, ka, rk = vecs
    w2, a2, g2 = mats
    args = [P, mu, w0, w2, a0, a2, g2, kk, ka, rk, bd]
    in_specs = [pl.BlockSpec((1, tq, RWKV_PAD), lambda i, j: (i, j, 0))] + [full(a) for a in args[1:]]
    if has_vres:
        vf, v0, v1, v2 = vres
        args += [vf, v0, v1, v2]
        in_specs += [tok, full(v0), full(v1), full(v2)]
    out = jax.ShapeDtypeStruct((b, t, BRANCH_W), F32)
    out_t = jax.ShapeDtypeStruct((b, BRANCH_W, t), F32)
    tok_t = pl.BlockSpec((1, BRANCH_W, tq), lambda i, j: (i, 0, j))
    n_plain = 2 if has_vres else 3
    return pl.pallas_call(
        functools.partial(_prep_body, has_vres, tq),
        grid=(b, t // tq),
        in_specs=in_specs,
        out_specs=[tok_t] * 6 + [tok] * n_plain,
        out_shape=[out_t] * 6 + [out] * n_plain,
        scratch_shapes=[pltpu.VMEM((8, RWKV_COLS), F32)],
        compiler_params=_params("parallel", "arbitrary"),
        name="rwkv_prep",
    )(*args)


HALF_J = HEAD_DIM // 2
SUBLANES = 8


def _rec_body(tc, w_ref, nk_ref, b_ref, k_ref, r_ref, v_ref, after_ref, y_ref, s_scr, ypart_scr):
    del after_ref

    @pl.when(pl.program_id(0) == 0)
    def _():
        s_scr[...] = jnp.zeros_like(s_scr)

    groups = [pl.ds(g * SUBLANES, SUBLANES) for g in range(HEAD_DIM // SUBLANES)]

    def fold(x):
        return x + pltpu.roll(x, HEAD_DIM, 1)

    def step(t, carry):
        part = [None] * len(groups)
        for j in range(HALF_J):
            nk = nk_ref[t, pl.ds(j, 1), :]
            for g, rows in enumerate(groups):
                term = s_scr[j, rows, :] * nk
                part[g] = term if part[g] is None else part[g] + term
        sa = [fold(p) for p in part]
        v = [jnp.concatenate([v_ref[t, rows, :]] * 2, axis=-1) for rows in groups]
        yacc = [None] * len(groups)
        for j in range(HALF_J):
            w = w_ref[t, pl.ds(j, 1), :]
            bb = b_ref[t, pl.ds(j, 1), :]
            kk = k_ref[t, pl.ds(j, 1), :]
            rr = r_ref[t, pl.ds(j, 1), :]
            for g, rows in enumerate(groups):
                sn = s_scr[j, rows, :] * w + (sa[g] * bb + v[g] * kk)
                s_scr[j, rows, :] = sn
                term = sn * rr
                yacc[g] = term if yacc[g] is None else yacc[g] + term
        for g, rows in enumerate(groups):
            ypart_scr[t, rows, :] = yacc[g]
        return carry

    lax.fori_loop(0, tc, step, 0)
    y = fold(ypart_scr[...].reshape(tc * HEAD_DIM, LANES))
    y_ref[...] = y[:, :HEAD_DIM].reshape(tc, HEAD_DIM, HEAD_DIM)


def _recurrence(dec, nkk, bb, k2, r, v, after, tc=32):
    t = dec.shape[0]
    jspec = pl.BlockSpec((tc, HALF_J, LANES), lambda i: (i, 0, 0))
    ispec = pl.BlockSpec((tc, HEAD_DIM, HEAD_DIM), lambda i: (i, 0, 0))
    return pl.pallas_call(
        functools.partial(_rec_body, tc),
        grid=(t // tc,),
        in_specs=[jspec] * 5 + [ispec, pl.BlockSpec(memory_space=pl.ANY)],
        out_specs=ispec,
        out_shape=jax.ShapeDtypeStruct((t, HEAD_DIM, HEAD_DIM), F32),
        scratch_shapes=[pltpu.VMEM((HALF_J, HEAD_DIM, LANES), F32), pltpu.VMEM((tc, HEAD_DIM, LANES), F32)],
        compiler_params=_params("arbitrary"),
        cost_estimate=pl.CostEstimate(flops=9 * t * HALF_J * HEAD_DIM * LANES, transcendentals=0,
                                      bytes_accessed=4 * t * LANES * (5 * HALF_J + 2 * HEAD_DIM)),
        name="rwkv_rec",
    )(dec, nkk, bb, k2, r, v, after)


def _to_val_lanes(a_t):
    b, _, t = a_t.shape
    return a_t.reshape(b, N_HEADS, HEAD_DIM, t).transpose(3, 2, 0, 1).reshape(t, HEAD_DIM, b * N_HEADS)


def _to_key_lanes(a_t):
    b, _, t = a_t.shape
    a = a_t.reshape(b, N_HEADS, 2, HALF_J, t).transpose(4, 3, 2, 0, 1)
    return a.reshape(t, HALF_J, 2 * b * N_HEADS)


def _from_val_lanes(y, b):
    t = y.shape[0]
    return y.reshape(t, HEAD_DIM, b, N_HEADS).transpose(2, 3, 1, 0).reshape(b, BRANCH_W, t)


def _sb_body(tq, tk, q_ref, k_ref, v_ref, qg_ref, kg_ref, u_ref, o_ref,
             kn_scr, vb_scr, qn_scr, hl_scr, lsig_scr, logit_scr, run_scr, acc_scr):
    qi = pl.program_id(2)
    nsub = tq // tk
    nblk = kn_scr.shape[0]
    lane = lax.broadcasted_iota(jnp.int32, (1, LANES), 1)
    first = lane < HEAD_DIM
    m_a = first.astype(F32)
    m_b = 1.0 - m_a

    def headnorm(x, g):
        x2 = x * x
        s_a = jnp.sum(x2 * m_a, axis=-1, keepdims=True)
        s_b = jnp.sum(x2 * m_b, axis=-1, keepdims=True)
        ms = jnp.where(first, s_a, s_b) * (1.0 / HEAD_DIM)
        return x * lax.rsqrt(ms + NORM_EPS) * g

    @pl.when(qi == 0)
    def _():
        kn = headnorm(k_ref[0].astype(F32), kg_ref[...])
        v = v_ref[0].astype(F32)
        kn_scr[:, :tk, :] = (kn * m_a).astype(BF16).reshape(nblk, tk, LANES)
        kn_scr[:, tk:, :] = (kn * m_b).astype(BF16).reshape(nblk, tk, LANES)
        vb_scr[:, :tk, :] = (v * m_a).astype(BF16).reshape(nblk, tk, LANES)
        vb_scr[:, tk:, :] = (v * m_b).astype(BF16).reshape(nblk, tk, LANES)

    qn_scr[...] = (headnorm(q_ref[0].astype(F32), qg_ref[...]) * (LOG2E * HEAD_DIM ** -0.5)).astype(BF16)
    run_scr[...] = jnp.zeros_like(run_scr)
    acc_scr[...] = jnp.zeros_like(acc_scr)
    uu = u_ref[...]
    rowi = lax.broadcasted_iota(jnp.int32, (SB_ROWS, 2 * tk), 0)
    coli = lax.broadcasted_iota(jnp.int32, (SB_ROWS, 2 * tk), 1) & (tk - 1)

    last = qi * nsub + (nsub - 1)
    heads = (slice(0, tk), slice(tk, 2 * tk))
    groups = [slice(s * SB_ROWS, (s + 1) * SB_ROWS) for s in range(tq // SB_ROWS)]

    def score(m, masked, rows):
        z = lax.dot_general(qn_scr[rows, :], kn_scr[jnp.maximum(last - m, 0)], NT_DIMS,
                            preferred_element_type=F32)
        l1m = -(jnp.maximum(z, 0.0) + jnp.log2(1.0 + jnp.exp2(-jnp.abs(z))))
        lsig = z + l1m
        if masked:
            mask = (coli + ((nsub - 1 - m) * tk - rows.start)) < rowi
            l1m = jnp.where(mask, l1m, 0.0)
            lsig = jnp.where(mask, lsig, MASKED_LOGIT)
        hi = l1m.astype(BF16)
        lo = (l1m - hi.astype(F32)).astype(BF16)
        for h, sl in enumerate(heads):
            hl_scr[h, rows, :] = jnp.concatenate([hi[:, sl], lo[:, sl]], axis=1)
        lsig_scr[rows, :] = lsig

    def suffix(rows):
        for h, sl in enumerate(heads):
            ce = _dot(hl_scr[h, rows, :], uu)
            run = run_scr[h, rows, :]
            logit_scr[rows, sl] = lsig_scr[rows, sl] + ce[:, :tk] + run
            run_scr[h, rows, :] = run + ce[:, tk:]

    def weigh(m, rows):
        acc_scr[rows, :] += _dot(jnp.exp2(logit_scr[rows, :]).astype(BF16), vb_scr[last - m])

    def body(m, carry):
        for rows in groups:
            weigh(m, rows)
            suffix(rows)
            score(m + 2, False, rows)
        return carry

    def live(rows, m):
        return (nsub - 1 - m) * tk <= rows.stop - 2

    def cut(rows, m):
        return (nsub - m) * tk - 1 >= rows.start

    for rows in groups:
        if live(rows, 0):
            score(0, cut(rows, 0), rows)
    for rows in groups:
        if live(rows, 0):
            suffix(rows)
        if live(rows, 1):
            score(1, cut(rows, 1), rows)
    for m in range(nsub):
        for rows in groups:
            if live(rows, m):
                weigh(m, rows)
            if live(rows, m + 1):
                suffix(rows)
            if live(rows, m + 2):
                score(m + 2, cut(rows, m + 2), rows)
    lax.fori_loop(nsub, (qi + 1) * nsub, body, 0)
    o_ref[0] = acc_scr[...]


def _sb_attention(P, qg, kg, uu, tq=1024, tk=LANES):
    b, t, _ = P.shape
    tq = min(tq, t)
    npair = N_HEADS // 2
    qoff, koff, voff = 0, BRANCH_W // LANES, 2 * BRANCH_W // LANES
    stacked = pltpu.VMEM((t // tk, 2 * tk, LANES), BF16)
    return pl.pallas_call(
        functools.partial(_sb_body, tq, tk),
        grid=(b, npair, t // tq),
        in_specs=[
            pl.BlockSpec((1, tq, LANES), lambda i, h, j: (i, j, qoff + h)),
            pl.BlockSpec((1, t, LANES), lambda i, h, j: (i, 0, koff + h)),
            pl.BlockSpec((1, t, LANES), lambda i, h, j: (i, 0, voff + h)),
            pl.BlockSpec((1, LANES), lambda i, h, j: (0, 0)),
            pl.BlockSpec((1, LANES), lambda i, h, j: (0, 0)),
            pl.BlockSpec((2 * tk, 2 * tk), lambda i, h, j: (0, 0)),
        ],
        out_specs=pl.BlockSpec((1, tq, LANES), lambda i, h, j: (i, j, h)),
        out_shape=jax.ShapeDtypeStruct((b, t, BRANCH_W), F32),
        scratch_shapes=[stacked, stacked,
                        pltpu.VMEM((tq, LANES), BF16),
                        pltpu.VMEM((2, tq, 2 * tk), BF16),
                        pltpu.VMEM((tq, 2 * tk), F32),
                        pltpu.VMEM((tq, 2 * tk), F32),
                        pltpu.VMEM((2, tq, LANES), F32),
                        pltpu.VMEM((tq, LANES), F32)],
        compiler_params=_params("parallel", "parallel", "arbitrary"),
        cost_estimate=pl.CostEstimate(flops=b * N_HEADS * t * t * (2 * HEAD_DIM + 4 * tk + 2 * HEAD_DIM) // 2,
                                      transcendentals=3 * b * N_HEADS * t * t // 2,
                                      bytes_accessed=2 * 3 * b * t * BRANCH_W + 4 * b * t * BRANCH_W),
        name="sb_attn",
    )(P, P, P, qg, kg, uu)


def _memkv_body(m_ref, g_ref, w_ref, kg_ref, k_o, v_o):
    kv = _dot(_rms(m_ref[0], g_ref[...]).astype(BF16), w_ref[...])
    for h in range(MEM_HEADS):
        sl = slice(h * MEM_HD, (h + 1) * MEM_HD)
        k_o[0, :, sl] = _rms(kv[:, sl], kg_ref[...]).astype(BF16)
    v_o[0] = kv[:, BRANCH_W:].astype(BF16)


def _mem_kv(mem, g, w_bf16, kg):
    b = mem.shape[0]
    out = jax.ShapeDtypeStruct((b, MEM_LEN, BRANCH_W), BF16)
    blk = pl.BlockSpec((1, MEM_LEN, BRANCH_W), lambda i: (i, 0, 0))
    return pl.pallas_call(
        _memkv_body,
        grid=(b,),
        in_specs=[
            pl.BlockSpec((1, MEM_LEN, D_MODEL), lambda i: (i, 0, 0)),
            pl.BlockSpec((1, D_MODEL), lambda i: (0, 0)),
            pl.BlockSpec((D_MODEL, 2 * BRANCH_W), lambda i: (0, 0)),
            pl.BlockSpec((1, MEM_HD), lambda i: (0, 0)),
        ],
        out_specs=[blk, blk],
        out_shape=[out, out],
        compiler_params=_params("parallel"),
        name="mem_kv",
    )(mem, g, w_bf16, kg)


def _memattn_body(q_ref, k_ref, v_ref, qg_ref, o_ref):
    q = q_ref[0].astype(F32)
    for h in range(MEM_HEADS):
        sl = slice(h * MEM_HD, (h + 1) * MEM_HD)
        qh = (_rms(q[:, sl], qg_ref[...]) * (MEM_HD ** -0.5)).astype(BF16)
        s = lax.dot_general(qh, k_ref[0, :, sl], NT_DIMS, preferred_element_type=F32)
        p = jnp.exp(s - jnp.max(s, axis=-1, keepdims=True))
        den = jnp.sum(p, axis=-1, keepdims=True)
        o_ref[0, :, sl] = _dot(p.astype(BF16), v_ref[0, :, sl]) / den


def _mem_attention(P, kn, mv, qg, tq=512):
    b, t, _ = P.shape
    qoff = MEM_Q_OFF // BRANCH_W
    kv = pl.BlockSpec((1, MEM_LEN, BRANCH_W), lambda i, j: (i, 0, 0))
    return pl.pallas_call(
        _memattn_body,
        grid=(b, t // tq),
        in_specs=[
            pl.BlockSpec((1, tq, BRANCH_W), lambda i, j: (i, j, qoff)),
            kv, kv,
            pl.BlockSpec((1, MEM_HD), lambda i, j: (0, 0)),
        ],
        out_specs=pl.BlockSpec((1, tq, BRANCH_W), lambda i, j: (i, j, 0)),
        out_shape=jax.ShapeDtypeStruct((b, t, BRANCH_W), F32),
        compiler_params=_params("parallel", "parallel"),
        name="mem_attn",
    )(P, kn, mv, qg)


def _merge_body(x_ref, y_ref, bonus_ref, g_ref, yb_ref, ym_ref, g0_ref, g1_ref, g2_ref,
                lng_ref, lnb_ref, bd_ref, wbr_ref, wout_ref, o_ref):
    bd = bd_ref[...]
    y = y_ref[0].T
    d = y - _dot_ones(y, bd) * (1.0 / HEAD_DIM)
    var = _dot_ones(d * d, bd) * (1.0 / HEAD_DIM)
    ya = (d * lax.rsqrt(var + LNX_EPS) * lng_ref[...] + lnb_ref[...] + bonus_ref[0]) * g_ref[0]
    m = (_sigmoid(g0_ref[0].astype(F32)) * _dot(ya.astype(BF16), wbr_ref[0])
         + _sigmoid(g1_ref[0].astype(F32)) * _dot(yb_ref[0].astype(BF16), wbr_ref[1])
         + _sigmoid(g2_ref[0].astype(F32)) * _dot(ym_ref[0].astype(BF16), wbr_ref[2]))
    o_ref[0] = x_ref[0] + _dot(m.astype(BF16), wout_ref[...])


def _merge(x, y, bonus, g, yb, ym, gates, lng, lnb, bd, wbr, wout, tq=256):
    b, t, _ = x.shape
    tok = pl.BlockSpec((1, tq, BRANCH_W), lambda i, j: (i, j, 0))
    xblk = pl.BlockSpec((1, tq, D_MODEL), lambda i, j: (i, j, 0))
    goff = GATE_OFF // D_MODEL

    def gate(n):
        return pl.BlockSpec((1, tq, D_MODEL), lambda i, j: (i, j, goff + n))

    def full(a):
        return pl.BlockSpec(a.shape, lambda i, j: (0,) * a.ndim)

    return pl.pallas_call(
        _merge_body,
        grid=(b, t // tq),
        in_specs=[xblk, pl.BlockSpec((1, BRANCH_W, tq), lambda i, j: (i, 0, j)), tok, tok, tok, tok,
                  gate(0), gate(1), gate(2),
                  full(lng), full(lnb), full(bd), full(wbr), full(wout)],
        out_specs=xblk,
        out_shape=jax.ShapeDtypeStruct(x.shape, F32),
        compiler_params=_params("parallel", "parallel"),
        name="merge",
    )(x, y, bonus, g, yb, ym, gates, gates, gates, lng, lnb, bd, wbr, wout)


FF_TILE = 1408


def _ffn_body(tq, x_ref, g_ref, wup_ref, cw_ref, cb_ref, wdn_ref, o_ref, carry):
    @pl.when(pl.program_id(1) == 0)
    def _():
        carry[...] = jnp.zeros_like(carry)

    x = x_ref[0]
    h = _rms(x, g_ref[...]).astype(BF16)
    row = lax.broadcasted_iota(jnp.int32, (tq, 1), 0)
    acc = jnp.zeros((tq, D_MODEL), F32)
    for f in range(D_FF // FF_TILE):
        halves = []
        for half in range(2):
            c0 = half * D_FF + f * FF_TILE
            sl = slice(c0, c0 + FF_TILE)
            hh = _dot(h, wup_ref[:, sl])
            cp = carry[:, sl]
            s1 = jnp.where(row == 0, cp[7:8], pltpu.roll(hh, 1, 0))
            s2 = jnp.where(row == 0, cp[6:7], jnp.where(row == 1, cp[7:8], pltpu.roll(hh, 2, 0)))
            carry[:, sl] = hh[tq - 8:, :]
            cw = cw_ref[:, sl]
            halves.append(cb_ref[:, sl] + s2 * cw[0:1] + s1 * cw[1:2] + hh * cw[2:3])
        gate, val = halves
        u = gate * _sigmoid(gate) * val
        acc = acc + _dot(u.astype(BF16), wdn_ref[f * FF_TILE:(f + 1) * FF_TILE, :])
    o_ref[0] = x + acc


def _ffn(x, g, wup, cw, cb, wdn, tq=512):
    b, t, _ = x.shape
    xblk = pl.BlockSpec((1, tq, D_MODEL), lambda i, j: (i, j, 0))

    def full(a):
        return pl.BlockSpec(a.shape, lambda i, j: (0,) * a.ndim, pipeline_mode=pl.Buffered(1))

    return pl.pallas_call(
        functools.partial(_ffn_body, tq),
        grid=(b, t // tq),
        in_specs=[xblk, full(g), full(wup), full(cw), full(cb), full(wdn)],
        out_specs=xblk,
        out_shape=jax.ShapeDtypeStruct(x.shape, F32),
        scratch_shapes=[pltpu.VMEM((8, 2 * D_FF), F32)],
        compiler_params=_params("parallel", "arbitrary"),
        name="conv_ffn",
    )(x, g, wup, cw, cb, wdn)


def _row(v):
    return v.reshape(1, -1)


def kernel(x, mem, norm1_g, w_in, shift_mu, decay_w0, decay_w2, iclr_a0, iclr_a2, gate_g2, k_k, k_a, r_k, lnx_g, lnx_b, vres_v0, vres_v1, vres_v2, sb_q_norm_g, sb_k_norm_g, mem_norm_g, w_mem_kv, mem_q_norm_g, mem_k_norm_g, w_branch, w_out, norm2_g, w_up, conv_w, conv_b, w_down):
    b, t, _ = x.shape
    assert 2 * b * N_HEADS == LANES, "recurrence layout packs (key-half, batch, head) onto the lanes"
    depth = w_in.shape[0]

    head_id = jnp.arange(BRANCH_W) // HEAD_DIM
    bd = (head_id[:, None] == head_id[None, :]).astype(BF16)
    tri = jnp.arange(LANES)
    u_ext = jnp.concatenate([(tri[:, None] > tri[None, :]).astype(BF16),
                             jnp.ones((LANES, LANES), BF16)], axis=1)
    u_ext = jnp.concatenate([u_ext, u_ext], axis=0)
    zpad64 = jnp.zeros((64, BRANCH_W), F32)

    v_first = None
    for l in range(depth):
        x2d = x.reshape(b * t, D_MODEL)
        w_rwkv = jnp.pad(w_in[l][:, :RWKV_COLS], ((0, 0), (0, RWKV_PAD - RWKV_COLS))).astype(BF16)
        P, Pr = _proj(x2d, _row(norm1_g[l]), w_rwkv, w_in[l][:, RWKV_COLS:].astype(BF16))
        P, Pr = P.reshape(b, t, RWKV_PAD), Pr.reshape(b, t, REST_COLS)

        vecs = (_row(shift_mu[l]), _row(decay_w0[l]), _row(iclr_a0[l]), _row(k_k[l]), _row(k_a[l]), _row(r_k[l]))
        mats = (jnp.concatenate([decay_w2[l], zpad64], axis=0).astype(BF16),
                jnp.concatenate([zpad64, iclr_a2[l]], axis=0).astype(BF16),
                gate_g2[l].astype(BF16))
        vres = None
        if l > 0:
            v1 = jnp.pad(vres_v1[l - 1], ((0, 0), (0, LANES - vres_v1.shape[-1]))).astype(BF16)
            v2 = jnp.pad(vres_v2[l - 1], ((0, LANES - vres_v2.shape[-2]), (0, 0))).astype(BF16)
            vres = (v_first, _row(vres_v0[l - 1]), v1, v2)
        r, dec, k2, v, nkk, bb, g, bonus, *v_plain = _prep(P, vecs, mats, bd, vres)
        if l == 0:
            v_first = v_plain[0]

        kn, mv = _mem_kv(mem, _row(mem_norm_g[l]), w_mem_kv[l].astype(BF16), _row(mem_k_norm_g[l]))
        ym = _mem_attention(Pr, kn, mv, _row(mem_q_norm_g[l]))

        y = _recurrence(_to_key_lanes(dec), _to_key_lanes(nkk), _to_key_lanes(bb), _to_key_lanes(k2),
                        _to_key_lanes(r), _to_val_lanes(v), ym)
        y = _from_val_lanes(y, b)

        qg2 = _row(jnp.concatenate([sb_q_norm_g[l], sb_q_norm_g[l]]))
        kg2 = _row(jnp.concatenate([sb_k_norm_g[l], sb_k_norm_g[l]]))
        yb = _sb_attention(Pr, qg2, kg2, u_ext)

        x = _merge(x, y, bonus, g, yb, ym, Pr, _row(lnx_g[l]), _row(lnx_b[l]), bd,
                   w_branch[l].astype(BF16), w_out[l].astype(BF16))
        x = _ffn(x, _row(norm2_g[l]), w_up[l].astype(BF16), conv_w[l], _row(conv_b[l]), w_down[l].astype(BF16))
    return x
```

```python
import functools

import jax
import jax.numpy as jnp
from jax import lax
from jax.experimental import pallas as pl
from jax.experimental.pallas import tpu as pltpu

F32 = jnp.float32
BF16 = jnp.bfloat16

D_MODEL = 1024
N_HEADS = 8
HEAD_DIM = 64
BRANCH_W = 512
RWKV_COLS = 1792
RWKV_PAD = 2048
REST_COLS = 5120
MEM_Q_OFF = 1536
GATE_OFF = 2048
LOG2E = 1.4426950408889634
MEM_LEN = 256
MEM_HEADS = 4
MEM_HD = 128
D_FF = 2816
NORM_EPS = 1e-6
LNX_EPS = 64e-5
LANES = 128
MASKED_LOGIT = -1e30
SB_ROWS = 128
VMEM_LIMIT = 56 * 1024 * 1024

NT_DIMS = (((1,), (1,)), ((), ()))


def _rms(xf, g, eps=NORM_EPS):
    return xf * lax.rsqrt(jnp.mean(xf * xf, axis=-1, keepdims=True) + eps) * g


def _sigmoid(x):
    return 1.0 / (1.0 + jnp.exp(-x))


def _softplus(x):
    return jnp.maximum(x, 0.0) + jnp.log(1.0 + jnp.exp(-jnp.abs(x)))


def _dot(a, b):
    return jnp.dot(a, b, preferred_element_type=F32)


def _split3(x):
    h1 = x.astype(BF16)
    r1 = x - h1.astype(F32)
    h2 = r1.astype(BF16)
    h3 = (r1 - h2.astype(F32)).astype(BF16)
    return h1, h2, h3


def _dot_ones(x, ones_bf16):
    h1, h2, h3 = _split3(x)
    return _dot(h1, ones_bf16) + _dot(h2, ones_bf16) + _dot(h3, ones_bf16)


def _params(*sem):
    return pltpu.CompilerParams(dimension_semantics=sem, vmem_limit_bytes=VMEM_LIMIT)


def _cast_body(w_ref, o_ref):
    o_ref[...] = w_ref[...].astype(BF16)


def _to_bf16(w, l, col0=0, ncols=None, out_cols=None, blk=256):
    _, rows, total = w.shape
    ncols = total - col0 if ncols is None else ncols
    out_cols = ncols if out_cols is None else out_cols
    nreal = ncols // blk

    def body(w_ref, o_ref):
        @pl.when(pl.program_id(0) < nreal)
        def _():
            o_ref[...] = w_ref[...].astype(BF16)

        @pl.when(pl.program_id(0) >= nreal)
        def _():
            o_ref[...] = jnp.zeros_like(o_ref)

    return pl.pallas_call(
        body if out_cols > ncols else _cast_body,
        grid=(out_cols // blk,),
        in_specs=[pl.BlockSpec((None, rows, blk), lambda j: (l, 0, col0 // blk + jnp.minimum(j, nreal - 1)))],
        out_specs=pl.BlockSpec((rows, blk), lambda j: (0, j)),
        out_shape=jax.ShapeDtypeStruct((rows, out_cols), BF16),
        compiler_params=_params("parallel"),
        name="to_bf16",
    )(w)


def _proj_body(x_ref, g_ref, wa_ref, wb_ref, oa_ref, ob_ref):
    h = _rms(x_ref[...], g_ref[...]).astype(BF16)
    oa_ref[...] = _dot(h, wa_ref[...])
    ob_ref[...] = _dot(h, wb_ref[...]).astype(BF16)


def _proj(x2d, g, w_rwkv, w_rest, tq=512):
    n = x2d.shape[0]
    ca, cb = w_rwkv.shape[1], w_rest.shape[1]

    def full(a):
        return pl.BlockSpec(a.shape, lambda i: (0, 0), pipeline_mode=pl.Buffered(1))

    return pl.pallas_call(
        _proj_body,
        grid=(n // tq,),
        in_specs=[pl.BlockSpec((tq, D_MODEL), lambda i: (i, 0)), full(g), full(w_rwkv), full(w_rest)],
        out_specs=[pl.BlockSpec((tq, ca), lambda i: (i, 0)), pl.BlockSpec((tq, cb), lambda i: (i, 0))],
        out_shape=[jax.ShapeDtypeStruct((n, ca), F32), jax.ShapeDtypeStruct((n, cb), BF16)],
        compiler_params=_params("parallel"),
        cost_estimate=pl.CostEstimate(flops=2 * n * D_MODEL * (ca + cb), transcendentals=n,
                                      bytes_accessed=4 * n * D_MODEL + 2 * D_MODEL * (ca + cb) + n * (4 * ca + 2 * cb)),
        name="proj",
    )(x2d, g, w_rwkv, w_rest)


def _prep_body(has_vres, tq, *refs):
    if has_vres:
        (p_ref, mu_ref, w0_ref, w2_ref, a0_ref, a2_ref, g2_ref, kk_ref, ka_ref, rk_ref, bd_ref,
         vf_ref, v0_ref, v1_ref, v2_ref,
         r_o, dec_o, k_o, v_o, nkk_o, bb_o, g_o, bonus_o, carry) = refs
        vfirst_o = None
    else:
        (p_ref, mu_ref, w0_ref, w2_ref, a0_ref, a2_ref, g2_ref, kk_ref, ka_ref, rk_ref, bd_ref,
         r_o, dec_o, k_o, v_o, nkk_o, bb_o, g_o, bonus_o, vfirst_o, carry) = refs

    @pl.when(pl.program_id(1) == 0)
    def _():
        carry[...] = jnp.zeros_like(carry)

    p = p_ref[0][:, :RWKV_COLS]
    row = lax.broadcasted_iota(jnp.int32, (tq, 1), 0)
    prev = jnp.where(row == 0, carry[7:8, :], pltpu.roll(p, 1, 0))
    carry[...] = p[tq - 8:, :]
    ps = p + (prev - p) * mu_ref[...]

    r = ps[:, 0:512]
    k = ps[:, 512:1024]
    v = ps[:, 1024:1536]
    wa = ps[:, 1536:1664]
    gl = ps[:, 1664:1792]

    wlog = -_softplus(-(w0_ref[...] + _dot(jnp.tanh(wa).astype(BF16), w2_ref[...]))) - 0.5
    dec = jnp.exp(-jnp.exp(wlog))
    a = _sigmoid(a0_ref[...] + _dot(wa.astype(BF16), a2_ref[...]))
    g = _dot(_sigmoid(gl).astype(BF16), g2_ref[...])
    if has_vres:
        lo = _dot(v.astype(BF16), v1_ref[...])
        v = v + (vf_ref[0] - v) * _sigmoid(v0_ref[...] + _dot(lo.astype(BF16), v2_ref[...]))
    bd = bd_ref[...]
    kk = k * kk_ref[...]
    kk = kk / jnp.maximum(jnp.sqrt(_dot_ones(kk * kk, bd)), 1e-12)
    k2 = k * (1.0 + (a - 1.0) * ka_ref[...])
    bonus = _dot_ones(r * k2 * rk_ref[...], bd) * v

    r_o[0] = r.T
    dec_o[0] = dec.T
    k_o[0] = k2.T
    v_o[0] = v.T
    nkk_o[0] = (-kk).T
    bb_o[0] = (kk * a).T
    g_o[0] = g
    bonus_o[0] = bonus
    if vfirst_o is not None:
        vfirst_o[0] = v


def _prep(P, vecs, mats, bd, vres, tq=256):
    b, t, _ = P.shape
    has_vres = vres is not None
    tok = pl.BlockSpec((1, tq, BRANCH_W), lambda i, j: (i, j, 0))

    def full(a):
        return pl.BlockSpec(a.shape, lambda i, j: (0,) * a.ndim)

    mu, w0, a0, kk, ka, rk = vecs
    w2, a2, g2 = mats
    args = [P, mu, w0, w2, a0, a2, g2, kk, ka, rk, bd]
    in_specs = [pl.BlockSpec((1, tq, RWKV_PAD), lambda i, j: (i, j, 0))] + [full(a) for a in args[1:]]
    if has_vres:
        vf, v0, v1, v2 = vres
        args += [vf, v0, v1, v2]
        in_specs += [tok, full(v0), full(v1), full(v2)]
    out = jax.ShapeDtypeStruct((b, t, BRANCH_W), F32)
    out_t = jax.ShapeDtypeStruct((b, BRANCH_W, t), F32)
    tok_t = pl.BlockSpec((1, BRANCH_W, tq), lambda i, j: (i, 0, j))
    n_plain = 2 if has_vres else 3
    return pl.pallas_call(
        functools.partial(_prep_body, has_vres, tq),
        grid=(b, t // tq),
        in_specs=in_specs,
        out_specs=[tok_t] * 6 + [tok] * n_plain,
        out_shape=[out_t] * 6 + [out] * n_plain,
        scratch_shapes=[pltpu.VMEM((8, RWKV_COLS), F32)],
        compiler_params=_params("parallel", "arbitrary"),
        name="rwkv_prep",
    )(*args)


HALF_J = HEAD_DIM // 2
SUBLANES = 8


def _rec_body(tc, w_ref, nk_ref, b_ref, k_ref, r_ref, v_ref, after_ref, y_ref, s_scr, ypart_scr):
    del after_ref

    @pl.when(pl.program_id(0) == 0)
    def _():
        s_scr[...] = jnp.zeros_like(s_scr)

    groups = [pl.ds(g * SUBLANES, SUBLANES) for g in range(HEAD_DIM // SUBLANES)]

    def fold(x):
        return x + pltpu.roll(x, HEAD_DIM, 1)

    def step(t, carry):
        part = [None] * len(groups)
        for j in range(HALF_J):
            nk = nk_ref[t, pl.ds(j, 1), :]
            for g, rows in enumerate(groups):
                term = s_scr[j, rows, :] * nk
                part[g] = term if part[g] is None else part[g] + term
        sa = [fold(p) for p in part]
        v = [jnp.concatenate([v_ref[t, rows, :]] * 2, axis=-1) for rows in groups]
        yacc = [None] * len(groups)
        for j in range(HALF_J):
            w = w_ref[t, pl.ds(j, 1), :]
            bb = b_ref[t, pl.ds(j, 1), :]
            kk = k_ref[t, pl.ds(j, 1), :]
            rr = r_ref[t, pl.ds(j, 1), :]
            for g, rows in enumerate(groups):
                sn = s_scr[j, rows, :] * w + (sa[g] * bb + v[g] * kk)
                s_scr[j, rows, :] = sn
                term = sn * rr
                yacc[g] = term if yacc[g] is None else yacc[g] + term
        for g, rows in enumerate(groups):
            ypart_scr[t, rows, :] = yacc[g]
        return carry

    lax.fori_loop(0, tc, step, 0)
    y = fold(ypart_scr[...].reshape(tc * HEAD_DIM, LANES))
    y_ref[...] = y[:, :HEAD_DIM].reshape(tc, HEAD_DIM, HEAD_DIM)


def _recurrence(dec, nkk, bb, k2, r, v, after, tc=32):
    t = dec.shape[0]
    jspec = pl.BlockSpec((tc, HALF_J, LANES), lambda i: (i, 0, 0))
    ispec = pl.BlockSpec((tc, HEAD_DIM, HEAD_DIM), lambda i: (i, 0, 0))
    return pl.pallas_call(
        functools.partial(_rec_body, tc),
        grid=(t // tc,),
        in_specs=[jspec] * 5 + [ispec, pl.BlockSpec(memory_space=pl.ANY)],
        out_specs=ispec,
        out_shape=jax.ShapeDtypeStruct((t, HEAD_DIM, HEAD_DIM), F32),
        scratch_shapes=[pltpu.VMEM((HALF_J, HEAD_DIM, LANES), F32), pltpu.VMEM((tc, HEAD_DIM, LANES), F32)],
        compiler_params=_params("arbitrary"),
        cost_estimate=pl.CostEstimate(flops=9 * t * HALF_J * HEAD_DIM * LANES, transcendentals=0,
                                      bytes_accessed=4 * t * LANES * (5 * HALF_J + 2 * HEAD_DIM)),
        name="rwkv_rec",
    )(dec, nkk, bb, k2, r, v, after)


def _to_val_lanes(a_t):
    b, _, t = a_t.shape
    return a_t.reshape(b, N_HEADS, HEAD_DIM, t).transpose(3, 2, 0, 1).reshape(t, HEAD_DIM, b * N_HEADS)


def _to_key_lanes(a_t):
    b, _, t = a_t.shape
    a = a_t.reshape(b, N_HEADS, 2, HALF_J, t).transpose(4, 3, 2, 0, 1)
    return a.reshape(t, HALF_J, 2 * b * N_HEADS)


def _from_val_lanes(y, b):
    t = y.shape[0]
    return y.reshape(t, HEAD_DIM, b, N_HEADS).transpose(2, 3, 1, 0).reshape(b, BRANCH_W, t)


def _sb_body(tq, tk, q_ref, k_ref, v_ref, qg_ref, kg_ref, u_ref, o_ref,
             kn_scr, vb_scr, qn_scr, hl_scr, lsig_scr, logit_scr, run_scr, acc_scr):
    qi = pl.program_id(2)
    nsub = tq // tk
    nblk = kn_scr.shape[0]
    lane = lax.broadcasted_iota(jnp.int32, (1, LANES), 1)
    first = lane < HEAD_DIM
    m_a = first.astype(F32)
    m_b = 1.0 - m_a

    def headnorm(x, g):
        x2 = x * x
        s_a = jnp.sum(x2 * m_a, axis=-1, keepdims=True)
        s_b = jnp.sum(x2 * m_b, axis=-1, keepdims=True)
        ms = jnp.where(first, s_a, s_b) * (1.0 / HEAD_DIM)
        return x * lax.rsqrt(ms + NORM_EPS) * g

    @pl.when(qi == 0)
    def _():
        kn = headnorm(k_ref[0].astype(F32), kg_ref[...])
        v = v_ref[0].astype(F32)
        kn_scr[:, :tk, :] = (kn * m_a).astype(BF16).reshape(nblk, tk, LANES)
        kn_scr[:, tk:, :] = (kn * m_b).astype(BF16).reshape(nblk, tk, LANES)
        vb_scr[:, :tk, :] = (v * m_a).astype(BF16).reshape(nblk, tk, LANES)
        vb_scr[:, tk:, :] = (v * m_b).astype(BF16).reshape(nblk, tk, LANES)

    qn_scr[...] = (headnorm(q_ref[0].astype(F32), qg_ref[...]) * (LOG2E * HEAD_DIM ** -0.5)).astype(BF16)
    run_scr[...] = jnp.zeros_like(run_scr)
    acc_scr[...] = jnp.zeros_like(acc_scr)
    uu = u_ref[...]
    rowi = lax.broadcasted_iota(jnp.int32, (SB_ROWS, 2 * tk), 0)
    coli = lax.broadcasted_iota(jnp.int32, (SB_ROWS, 2 * tk), 1) & (tk - 1)

    last = qi * nsub + (nsub - 1)
    heads = (slice(0, tk), slice(tk, 2 * tk))
    groups = [slice(s * SB_ROWS, (s + 1) * SB_ROWS) for s in range(tq // SB_ROWS)]

    def score(m, masked, rows):
        z = lax.dot_general(qn_scr[rows, :], kn_scr[jnp.maximum(last - m, 0)], NT_DIMS,
                            preferred_element_type=F32)
        l1m = -(jnp.maximum(z, 0.0) + jnp.log2(1.0 + jnp.exp2(-jnp.abs(z))))
        lsig = z + l1m
        if masked:
            mask = (coli + ((nsub - 1 - m) * tk - rows.start)) < rowi
            l1m = jnp.where(mask, l1m, 0.0)
            lsig = jnp.where(mask, lsig, MASKED_LOGIT)
        hi = l1m.astype(BF16)
        lo = (l1m - hi.astype(F32)).astype(BF16)
        for h, sl in enumerate(heads):
            hl_scr[h, rows, :] = jnp.concatenate([hi[:, sl], lo[:, sl]], axis=1)
        lsig_scr[rows, :] = lsig

    def suffix(rows):
        for h, sl in enumerate(heads):
            ce = _dot(hl_scr[h, rows, :], uu)
            run = run_scr[h, rows, :]
            logit_scr[rows, sl] = lsig_scr[rows, sl] + ce[:, :tk] + run
            run_scr[h, rows, :] = run + ce[:, tk:]

    def weigh(m, rows):
        acc_scr[rows, :] += _dot(jnp.exp2(logit_scr[rows, :]).astype(BF16), vb_scr[last - m])

    def body(m, carry):
        for rows in groups:
            weigh(m, rows)
            suffix(rows)
            score(m + 2, False, rows)
        return carry

    def live(rows, m):
        return (nsub - 1 - m) * tk <= rows.stop - 2

    def cut(rows, m):
        return (nsub - m) * tk - 1 >= rows.start

    for rows in groups:
        if live(rows, 0):
            score(0, cut(rows, 0), rows)
    for rows in groups:
        if live(rows, 0):
            suffix(rows)
        if live(rows, 1):
            score(1, cut(rows, 1), rows)
    for m in range(nsub):
        for rows in groups:
            if live(rows, m):
                weigh(m, rows)
            if live(rows, m + 1):
                suffix(rows)
            if live(rows, m + 2):
                score(m + 2, cut(rows, m + 2), rows)
    lax.fori_loop(nsub, (qi + 1) * nsub, body, 0)
    o_ref[0] = acc_scr[...]


def _sb_attention(P, qg, kg, uu, tq=1024, tk=LANES):
    b, t, _ = P.shape
    tq = min(tq, t)
    npair = N_HEADS // 2
    qoff, koff, voff = 0, BRANCH_W // LANES, 2 * BRANCH_W // LANES
    stacked = pltpu.VMEM((t // tk, 2 * tk, LANES), BF16)
    return pl.pallas_call(
        functools.partial(_sb_body, tq, tk),
        grid=(b, npair, t // tq),
        in_specs=[
            pl.BlockSpec((1, tq, LANES), lambda i, h, j: (i, j, qoff + h)),
            pl.BlockSpec((1, t, LANES), lambda i, h, j: (i, 0, koff + h)),
            pl.BlockSpec((1, t, LANES), lambda i, h, j: (i, 0, voff + h)),
            pl.BlockSpec((1, LANES), lambda i, h, j: (0, 0)),
            pl.BlockSpec((1, LANES), lambda i, h, j: (0, 0)),
            pl.BlockSpec((2 * tk, 2 * tk), lambda i, h, j: (0, 0)),
        ],
        out_specs=pl.BlockSpec((1, tq, LANES), lambda i, h, j: (i, j, h)),
        out_shape=jax.ShapeDtypeStruct((b, t, BRANCH_W), F32),
        scratch_shapes=[stacked, stacked,
                        pltpu.VMEM((tq, LANES), BF16),
                        pltpu.VMEM((2, tq, 2 * tk), BF16),
                        pltpu.VMEM((tq, 2 * tk), F32),
                        pltpu.VMEM((tq, 2 * tk), F32),
                        pltpu.VMEM((2, tq, LANES), F32),
                        pltpu.VMEM((tq, LANES), F32)],
        compiler_params=_params("parallel", "parallel", "arbitrary"),
        cost_estimate=pl.CostEstimate(flops=b * N_HEADS * t * t * (2 * HEAD_DIM + 4 * tk + 2 * HEAD_DIM) // 2,
                                      transcendentals=3 * b * N_HEADS * t * t // 2,
                                      bytes_accessed=2 * 3 * b * t * BRANCH_W + 4 * b * t * BRANCH_W),
        name="sb_attn",
    )(P, P, P, qg, kg, uu)


def _memkv_body(m_ref, g_ref, w_ref, kg_ref, k_o, v_o):
    kv = _dot(_rms(m_ref[0], g_ref[...]).astype(BF16), w_ref[...])
    for h in range(MEM_HEADS):
        sl = slice(h * MEM_HD, (h + 1) * MEM_HD)
        k_o[0, :, sl] = _rms(kv[:, sl], kg_ref[...]).astype(BF16)
    v_o[0] = kv[:, BRANCH_W:].astype(BF16)


def _mem_kv(mem, g, w_bf16, kg):
    b = mem.shape[0]
    out = jax.ShapeDtypeStruct((b, MEM_LEN, BRANCH_W), BF16)
    blk = pl.BlockSpec((1, MEM_LEN, BRANCH_W), lambda i: (i, 0, 0))
    return pl.pallas_call(
        _memkv_body,
        grid=(b,),
        in_specs=[
            pl.BlockSpec((1, MEM_LEN, D_MODEL), lambda i: (i, 0, 0)),
            pl.BlockSpec((1, D_MODEL), lambda i: (0, 0)),
            pl.BlockSpec((D_MODEL, 2 * BRANCH_W), lambda i: (0, 0)),
            pl.BlockSpec((1, MEM_HD), lambda i: (0, 0)),
        ],
        out_specs=[blk, blk],
        out_shape=[out, out],
        compiler_params=_params("parallel"),
        name="mem_kv",
    )(mem, g, w_bf16, kg)


def _memattn_body(q_ref, k_ref, v_ref, qg_ref, o_ref):
    q = q_ref[0].astype(F32)
    for h in range(MEM_HEADS):
        sl = slice(h * MEM_HD, (h + 1) * MEM_HD)
        qh = (_rms(q[:, sl], qg_ref[...]) * (MEM_HD ** -0.5)).astype(BF16)
        s = lax.dot_general(qh, k_ref[0, :, sl], NT_DIMS, preferred_element_type=F32)
        p = jnp.exp(s - jnp.max(s, axis=-1, keepdims=True))
        den = jnp.sum(p, axis=-1, keepdims=True)
        o_ref[0, :, sl] = _dot(p.astype(BF16), v_ref[0, :, sl]) / den


def _mem_attention(P, kn, mv, qg, tq=512):
    b, t, _ = P.shape
    qoff = MEM_Q_OFF // BRANCH_W
    kv = pl.BlockSpec((1, MEM_LEN, BRANCH_W), lambda i, j: (i, 0, 0))
    return pl.pallas_call(
        _memattn_body,
        grid=(b, t // tq),
        in_specs=[
            pl.BlockSpec((1, tq, BRANCH_W), lambda i, j: (i, j, qoff)),
            kv, kv,
            pl.BlockSpec((1, MEM_HD), lambda i, j: (0, 0)),
        ],
        out_specs=pl.BlockSpec((1, tq, BRANCH_W), lambda i, j: (i, j, 0)),
        out_shape=jax.ShapeDtypeStruct((b, t, BRANCH_W), F32),
        compiler_params=_params("parallel", "parallel"),
        name="mem_attn",
    )(P, kn, mv, qg)


def _merge_body(x_ref, y_ref, bonus_ref, g_ref, yb_ref, ym_ref, g0_ref, g1_ref, g2_ref,
                lng_ref, lnb_ref, bd_ref, wbr_ref, wout_ref, o_ref):
    bd = bd_ref[...]
    y = y_ref[0].T
    d = y - _dot_ones(y, bd) * (1.0 / HEAD_DIM)
    var = _dot_ones(d * d, bd) * (1.0 / HEAD_DIM)
    ya = (d * lax.rsqrt(var + LNX_EPS) * lng_ref[...] + lnb_ref[...] + bonus_ref[0]) * g_ref[0]
    m = (_sigmoid(g0_ref[0].astype(F32)) * _dot(ya.astype(BF16), wbr_ref[0])
         + _sigmoid(g1_ref[0].astype(F32)) * _dot(yb_ref[0].astype(BF16), wbr_ref[1])
         + _sigmoid(g2_ref[0].astype(F32)) * _dot(ym_ref[0].astype(BF16), wbr_ref[2]))
    o_ref[0] = x_ref[0] + _dot(m.astype(BF16), wout_ref[...])


def _merge(x, y, bonus, g, yb, ym, gates, lng, lnb, bd, wbr, wout, tq=512):
    b, t, _ = x.shape
    tok = pl.BlockSpec((1, tq, BRANCH_W), lambda i, j: (i, j, 0))
    xblk = pl.BlockSpec((1, tq, D_MODEL), lambda i, j: (i, j, 0))
    goff = GATE_OFF // D_MODEL

    def gate(n):
        return pl.BlockSpec((1, tq, D_MODEL), lambda i, j: (i, j, goff + n))

    def full(a):
        return pl.BlockSpec(a.shape, lambda i, j: (0,) * a.ndim)

    return pl.pallas_call(
        _merge_body,
        grid=(b, t // tq),
        in_specs=[xblk, pl.BlockSpec((1, BRANCH_W, tq), lambda i, j: (i, 0, j)), tok, tok, tok, tok,
                  gate(0), gate(1), gate(2),
                  full(lng), full(lnb), full(bd), full(wbr), full(wout)],
        out_specs=xblk,
        out_shape=jax.ShapeDtypeStruct(x.shape, F32),
        compiler_params=_params("parallel", "parallel"),
        name="merge",
    )(x, y, bonus, g, yb, ym, gates, gates, gates, lng, lnb, bd, wbr, wout)


FF_TILE = 1408


def _ffn_body(tq, x_ref, g_ref, wup_ref, cw_ref, cb_ref, wdn_ref, o_ref, carry):
    @pl.when(pl.program_id(1) == 0)
    def _():
        carry[...] = jnp.zeros_like(carry)

    x = x_ref[0]
    h = _rms(x, g_ref[...]).astype(BF16)
    row = lax.broadcasted_iota(jnp.int32, (tq, 1), 0)
    acc = jnp.zeros((tq, D_MODEL), F32)
    for f in range(D_FF // FF_TILE):
        halves = []
        for half in range(2):
            c0 = half * D_FF + f * FF_TILE
            sl = slice(c0, c0 + FF_TILE)
            hh = _dot(h, wup_ref[:, sl])
            cp = carry[:, sl]
            s1 = jnp.where(row == 0, cp[7:8], pltpu.roll(hh, 1, 0))
            s2 = jnp.where(row == 0, cp[6:7], jnp.where(row == 1, cp[7:8], pltpu.roll(hh, 2, 0)))
            carry[:, sl] = hh[tq - 8:, :]
            cw = cw_ref[:, sl]
            halves.append(cb_ref[:, sl] + s2 * cw[0:1] + s1 * cw[1:2] + hh * cw[2:3])
        gate, val = halves
        u = gate * _sigmoid(gate) * val
        acc = acc + _dot(u.astype(BF16), wdn_ref[f * FF_TILE:(f + 1) * FF_TILE, :])
    o_ref[0] = x + acc


def _ffn(x, g, wup, cw, cb, wdn, tq=512):
    b, t, _ = x.shape
    xblk = pl.BlockSpec((1, tq, D_MODEL), lambda i, j: (i, j, 0))

    def full(a):
        return pl.BlockSpec(a.shape, lambda i, j: (0,) * a.ndim, pipeline_mode=pl.Buffered(1))

    return pl.pallas_call(
        functools.partial(_ffn_body, tq),
        grid=(b, t // tq),
        in_specs=[xblk, full(g), full(wup), full(cw), full(cb), full(wdn)],
        out_specs=xblk,
        out_shape=jax.ShapeDtypeStruct(x.shape, F32),
        scratch_shapes=[pltpu.VMEM((8, 2 * D_FF), F32)],
        compiler_params=_params("parallel", "arbitrary"),
        name="conv_ffn",
    )(x, g, wup, cw, cb, wdn)


def _row(v):
    return v.reshape(1, -1)


def kernel(x, mem, norm1_g, w_in, shift_mu, decay_w0, decay_w2, iclr_a0, iclr_a2, gate_g2, k_k, k_a, r_k, lnx_g, lnx_b, vres_v0, vres_v1, vres_v2, sb_q_norm_g, sb_k_norm_g, mem_norm_g, w_mem_kv, mem_q_norm_g, mem_k_norm_g, w_branch, w_out, norm2_g, w_up, conv_w, conv_b, w_down):
    b, t, _ = x.shape
    assert 2 * b * N_HEADS == LANES, "recurrence layout packs (key-half, batch, head) onto the lanes"
    depth = w_in.shape[0]

    head_id = jnp.arange(BRANCH_W) // HEAD_DIM
    bd = (head_id[:, None] == head_id[None, :]).astype(BF16)
    tri = jnp.arange(LANES)
    u_ext = jnp.concatenate([(tri[:, None] > tri[None, :]).astype(BF16),
                             jnp.ones((LANES, LANES), BF16)], axis=1)
    u_ext = jnp.concatenate([u_ext, u_ext], axis=0)
    zpad64 = jnp.zeros((64, BRANCH_W), F32)

    v_first = None
    for l in range(depth):
        x2d = x.reshape(b * t, D_MODEL)
        w_rwkv = _to_bf16(w_in, l, 0, RWKV_COLS, RWKV_PAD)
        P, Pr = _proj(x2d, _row(norm1_g[l]), w_rwkv, _to_bf16(w_in, l, RWKV_COLS))
        P, Pr = P.reshape(b, t, RWKV_PAD), Pr.reshape(b, t, REST_COLS)

        vecs = (_row(shift_mu[l]), _row(decay_w0[l]), _row(iclr_a0[l]), _row(k_k[l]), _row(k_a[l]), _row(r_k[l]))
        mats = (jnp.concatenate([decay_w2[l], zpad64], axis=0).astype(BF16),
                jnp.concatenate([zpad64, iclr_a2[l]], axis=0).astype(BF16),
                gate_g2[l].astype(BF16))
        vres = None
        if l > 0:
            v1 = jnp.pad(vres_v1[l - 1], ((0, 0), (0, LANES - vres_v1.shape[-1]))).astype(BF16)
            v2 = jnp.pad(vres_v2[l - 1], ((0, LANES - vres_v2.shape[-2]), (0, 0))).astype(BF16)
            vres = (v_first, _row(vres_v0[l - 1]), v1, v2)
        r, dec, k2, v, nkk, bb, g, bonus, *v_plain = _prep(P, vecs, mats, bd, vres)
        if l == 0:
            v_first = v_plain[0]

        kn, mv = _mem_kv(mem, _row(mem_norm_g[l]), _to_bf16(w_mem_kv, l), _row(mem_k_norm_g[l]))
        ym = _mem_attention(Pr, kn, mv, _row(mem_q_norm_g[l]))

        y = _recurrence(_to_key_lanes(dec), _to_key_lanes(nkk), _to_key_lanes(bb), _to_key_lanes(k2),
                        _to_key_lanes(r), _to_val_lanes(v), ym)
        y = _from_val_lanes(y, b)

        qg2 = _row(jnp.concatenate([sb_q_norm_g[l], sb_q_norm_g[l]]))
        kg2 = _row(jnp.concatenate([sb_k_norm_g[l], sb_k_norm_g[l]]))
        yb = _sb_attention(Pr, qg2, kg2, u_ext)

        wbr = _to_bf16(w_branch.reshape(depth, -1, D_MODEL), l).reshape(w_branch.shape[1:])
        x = _merge(x, y, bonus, g, yb, ym, Pr, _row(lnx_g[l]), _row(lnx_b[l]), bd, wbr, _to_bf16(w_out, l))
        x = _ffn(x, _row(norm2_g[l]), _to_bf16(w_up, l), conv_w[l], _row(conv_b[l]), _to_bf16(w_down, l))
    return x
```

```python
import functools

import jax
import jax.numpy as jnp
from jax import lax
from jax.experimental import pallas as pl
from jax.experimental.pallas import tpu as pltpu

F32 = jnp.float32
BF16 = jnp.bfloat16

D_MODEL = 1024
N_HEADS = 8
HEAD_DIM = 64
BRANCH_W = 512
RWKV_COLS = 1792
RWKV_PAD = 2048
REST_COLS = 5120
MEM_Q_OFF = 1536
GATE_OFF = 2048
LOG2E = 1.4426950408889634
MEM_LEN = 256
MEM_HEADS = 4
MEM_HD = 128
D_FF = 2816
NORM_EPS = 1e-6
LNX_EPS = 64e-5
LANES = 128
MASKED_LOGIT = -1e30
SB_ROWS = 128
VMEM_LIMIT = 56 * 1024 * 1024

NT_DIMS = (((1,), (1,)), ((), ()))


def _rms(xf, g, eps=NORM_EPS):
    return xf * lax.rsqrt(jnp.mean(xf * xf, axis=-1, keepdims=True) + eps) * g


def _sigmoid(x):
    return 1.0 / (1.0 + jnp.exp(-x))


def _softplus(x):
    return jnp.maximum(x, 0.0) + jnp.log(1.0 + jnp.exp(-jnp.abs(x)))


def _dot(a, b):
    return jnp.dot(a, b, preferred_element_type=F32)


def _split3(x):
    h1 = x.astype(BF16)
    r1 = x - h1.astype(F32)
    h2 = r1.astype(BF16)
    h3 = (r1 - h2.astype(F32)).astype(BF16)
    return h1, h2, h3


def _dot_ones(x, ones_bf16):
    h1, h2, h3 = _split3(x)
    return _dot(h1, ones_bf16) + _dot(h2, ones_bf16) + _dot(h3, ones_bf16)


def _params(*sem):
    return pltpu.CompilerParams(dimension_semantics=sem, vmem_limit_bytes=VMEM_LIMIT)


def _cast_body(w_ref, o_ref):
    o_ref[...] = w_ref[...].astype(BF16)


def _to_bf16(w, l, col0=0, ncols=None, out_cols=None, blk=256):
    _, rows, total = w.shape
    ncols = total - col0 if ncols is None else ncols
    out_cols = ncols if out_cols is None else out_cols
    nreal = ncols // blk

    def body(w_ref, o_ref):
        @pl.when(pl.program_id(0) < nreal)
        def _():
            o_ref[...] = w_ref[...].astype(BF16)

        @pl.when(pl.program_id(0) >= nreal)
        def _():
            o_ref[...] = jnp.zeros_like(o_ref)

    return pl.pallas_call(
        body if out_cols > ncols else _cast_body,
        grid=(out_cols // blk,),
        in_specs=[pl.BlockSpec((None, rows, blk), lambda j: (l, 0, col0 // blk + jnp.minimum(j, nreal - 1)))],
        out_specs=pl.BlockSpec((rows, blk), lambda j: (0, j)),
        out_shape=jax.ShapeDtypeStruct((rows, out_cols), BF16),
        compiler_params=_params("parallel"),
        name="to_bf16",
    )(w)


def _proj_body(x_ref, g_ref, wa_ref, wb_ref, oa_ref, ob_ref):
    h = _rms(x_ref[...], g_ref[...]).astype(BF16)
    oa_ref[...] = _dot(h, wa_ref[...])
    ob_ref[...] = _dot(h, wb_ref[...]).astype(BF16)


def _proj(x2d, g, w_rwkv, w_rest, tq=512):
    n = x2d.shape[0]
    ca, cb = w_rwkv.shape[1], w_rest.shape[1]

    def full(a):
        return pl.BlockSpec(a.shape, lambda i: (0, 0), pipeline_mode=pl.Buffered(1))

    return pl.pallas_call(
        _proj_body,
        grid=(n // tq,),
        in_specs=[pl.BlockSpec((tq, D_MODEL), lambda i: (i, 0)), full(g), full(w_rwkv), full(w_rest)],
        out_specs=[pl.BlockSpec((tq, ca), lambda i: (i, 0)), pl.BlockSpec((tq, cb), lambda i: (i, 0))],
        out_shape=[jax.ShapeDtypeStruct((n, ca), F32), jax.ShapeDtypeStruct((n, cb), BF16)],
        compiler_params=_params("parallel"),
        cost_estimate=pl.CostEstimate(flops=2 * n * D_MODEL * (ca + cb), transcendentals=n,
                                      bytes_accessed=4 * n * D_MODEL + 2 * D_MODEL * (ca + cb) + n * (4 * ca + 2 * cb)),
        name="proj",
    )(x2d, g, w_rwkv, w_rest)


def _prep_body(has_vres, tq, *refs):
    if has_vres:
        (p_ref, mu_ref, w0_ref, w2_ref, a0_ref, a2_ref, g2_ref, kk_ref, ka_ref, rk_ref, bd_ref,
         vf_ref, v0_ref, v1_ref, v2_ref,
         r_o, dec_o, k_o, v_o, nkk_o, bb_o, g_o, bonus_o, carry) = refs
        vfirst_o = None
    else:
        (p_ref, mu_ref, w0_ref, w2_ref, a0_ref, a2_ref, g2_ref, kk_ref, ka_ref, rk_ref, bd_ref,
         r_o, dec_o, k_o, v_o, nkk_o, bb_o, g_o, bonus_o, vfirst_o, carry) = refs

    @pl.when(pl.program_id(1) == 0)
    def _():
        carry[...] = jnp.zeros_like(carry)

    p = p_ref[0][:, :RWKV_COLS]
    row = lax.broadcasted_iota(jnp.int32, (tq, 1), 0)
    prev = jnp.where(row == 0, carry[7:8, :], pltpu.roll(p, 1, 0))
    carry[...] = p[tq - 8:, :]
    ps = p + (prev - p) * mu_ref[...]

    r = ps[:, 0:512]
    k = ps[:, 512:1024]
    v = ps[:, 1024:1536]
    wa = ps[:, 1536:1664]
    gl = ps[:, 1664:1792]

    wlog = -_softplus(-(w0_ref[...] + _dot(jnp.tanh(wa).astype(BF16), w2_ref[...]))) - 0.5
    dec = jnp.exp(-jnp.exp(wlog))
    a = _sigmoid(a0_ref[...] + _dot(wa.astype(BF16), a2_ref[...]))
    g = _dot(_sigmoid(gl).astype(BF16), g2_ref[...])
    if has_vres:
        lo = _dot(v.astype(BF16), v1_ref[...])
        v = v + (vf_ref[0] - v) * _sigmoid(v0_ref[...] + _dot(lo.astype(BF16), v2_ref[...]))
    bd = bd_ref[...]
    kk = k * kk_ref[...]
    kk = kk / jnp.maximum(jnp.sqrt(_dot_ones(kk * kk, bd)), 1e-12)
    k2 = k * (1.0 + (a - 1.0) * ka_ref[...])
    bonus = _dot_ones(r * k2 * rk_ref[...], bd) * v

    def rec_layout(x):
        return x.T.reshape(HALF_J, 2, 1, N_HEADS, tq)

    r_o[...] = rec_layout(r)
    dec_o[...] = rec_layout(dec)
    k_o[...] = rec_layout(k2)
    v_o[0] = v.T
    nkk_o[...] = rec_layout(-kk)
    bb_o[...] = rec_layout(kk * a)
    g_o[0] = g.astype(BF16)
    bonus_o[0] = bonus.astype(BF16)
    if vfirst_o is not None:
        vfirst_o[0] = v


def _prep(P, vecs, mats, bd, vres, tq=256):
    b, t, _ = P.shape
    has_vres = vres is not None
    tok = pl.BlockSpec((1, tq, BRANCH_W), lambda i, j: (i, j, 0))

    def full(a):
        return pl.BlockSpec(a.shape, lambda i, j: (0,) * a.ndim)

    mu, w0, a0, kk, ka, rk = vecs
    w2, a2, g2 = mats
    args = [P, mu, w0, w2, a0, a2, g2, kk, ka, rk, bd]
    in_specs = [pl.BlockSpec((1, tq, RWKV_PAD), lambda i, j: (i, j, 0))] + [full(a) for a in args[1:]]
    if has_vres:
        vf, v0, v1, v2 = vres
        args += [vf, v0, v1, v2]
        in_specs += [tok, full(v0), full(v1), full(v2)]
    out = jax.ShapeDtypeStruct((b, t, BRANCH_W), F32)
    out_t = jax.ShapeDtypeStruct((HALF_J, 2, b, N_HEADS, t), F32)
    tok_t = pl.BlockSpec((HALF_J, 2, 1, N_HEADS, tq), lambda i, j: (0, 0, i, 0, j))
    out_v = jax.ShapeDtypeStruct((b, BRANCH_W, t), F32)
    tok_v = pl.BlockSpec((1, BRANCH_W, tq), lambda i, j: (i, 0, j))
    out_h = jax.ShapeDtypeStruct((b, t, BRANCH_W), BF16)
    return pl.pallas_call(
        functools.partial(_prep_body, has_vres, tq),
        grid=(b, t // tq),
        in_specs=in_specs,
        out_specs=[tok_t] * 3 + [tok_v] + [tok_t] * 2 + [tok] * (2 if has_vres else 3),
        out_shape=[out_t] * 3 + [out_v] + [out_t] * 2 + [out_h] * 2 + ([] if has_vres else [out]),
        scratch_shapes=[pltpu.VMEM((8, RWKV_COLS), F32)],
        compiler_params=_params("parallel", "arbitrary"),
        name="rwkv_prep",
    )(*args)


HALF_J = HEAD_DIM // 2
SUBLANES = 8


def _rec_body(tc, w_ref, nk_ref, b_ref, k_ref, r_ref, v_ref, after_ref, y_ref,
              s_scr, w_scr, nk_scr, b_scr, k_scr, r_scr, ypart_scr):
    del after_ref

    @pl.when(pl.program_id(0) == 0)
    def _():
        s_scr[...] = jnp.zeros_like(s_scr)

    for src, dst in ((w_ref, w_scr), (nk_ref, nk_scr), (b_ref, b_scr), (k_ref, k_scr), (r_ref, r_scr)):
        for j in range(HALF_J):
            dst[j] = src[j].T

    groups = [pl.ds(g * SUBLANES, SUBLANES) for g in range(HEAD_DIM // SUBLANES)]

    def fold(x):
        return x + pltpu.roll(x, HEAD_DIM, 1)

    def step(t, carry):
        part = [None] * len(groups)
        for j in range(HALF_J):
            nk = nk_scr[j, pl.ds(t, 1), :]
            for g, rows in enumerate(groups):
                term = s_scr[j, rows, :] * nk
                part[g] = term if part[g] is None else part[g] + term
        sa = [fold(p) for p in part]
        v = [jnp.concatenate([v_ref[t, rows, :]] * 2, axis=-1) for rows in groups]
        yacc = [None] * len(groups)
        for j in range(HALF_J):
            w = w_scr[j, pl.ds(t, 1), :]
            bb = b_scr[j, pl.ds(t, 1), :]
            kk = k_scr[j, pl.ds(t, 1), :]
            rr = r_scr[j, pl.ds(t, 1), :]
            for g, rows in enumerate(groups):
                sn = s_scr[j, rows, :] * w + (sa[g] * bb + v[g] * kk)
                s_scr[j, rows, :] = sn
                term = sn * rr
                yacc[g] = term if yacc[g] is None else yacc[g] + term
        for g, rows in enumerate(groups):
            ypart_scr[t, rows, :] = yacc[g]
        return carry

    lax.fori_loop(0, tc, step, 0)
    y = fold(ypart_scr[...].reshape(tc * HEAD_DIM, LANES))
    y_ref[...] = y[:, :HEAD_DIM].reshape(tc, HEAD_DIM, HEAD_DIM)


def _recurrence(dec, nkk, bb, k2, r, v, after, tc=128):
    t = dec.shape[-1]
    kspec = pl.BlockSpec((HALF_J, LANES, tc), lambda i: (0, 0, i))
    vspec = pl.BlockSpec((tc, HEAD_DIM, HEAD_DIM), lambda i: (i, 0, 0))
    turned = pltpu.VMEM((HALF_J, tc, LANES), F32)
    return pl.pallas_call(
        functools.partial(_rec_body, tc),
        grid=(t // tc,),
        in_specs=[kspec] * 5 + [vspec, pl.BlockSpec(memory_space=pl.ANY)],
        out_specs=vspec,
        out_shape=jax.ShapeDtypeStruct((t, HEAD_DIM, HEAD_DIM), F32),
        scratch_shapes=[pltpu.VMEM((HALF_J, HEAD_DIM, LANES), F32)] + [turned] * 5
                       + [pltpu.VMEM((tc, HEAD_DIM, LANES), F32)],
        compiler_params=_params("arbitrary"),
        cost_estimate=pl.CostEstimate(flops=9 * t * HALF_J * HEAD_DIM * LANES, transcendentals=0,
                                      bytes_accessed=4 * t * (5 * HALF_J * LANES + 2 * HEAD_DIM * HEAD_DIM)),
        name="rwkv_rec",
    )(dec, nkk, bb, k2, r, v, after)


def _value_lanes(v_t):
    b, _, t = v_t.shape
    return v_t.reshape(b, HALF_J, 2, N_HEADS, t).transpose(4, 2, 1, 0, 3).reshape(t, HEAD_DIM, b * N_HEADS)


def _from_value_lanes(y, b):
    t = y.shape[0]
    return y.reshape(t, 2, HALF_J, b, N_HEADS).transpose(3, 2, 1, 4, 0).reshape(b, BRANCH_W, t)


def _rec_order(a, axis=-1):
    a = jnp.moveaxis(a, axis, -1)
    lead = a.shape[:-1]
    a = jnp.swapaxes(a.reshape(lead + (N_HEADS, 2, HALF_J)), -1, -3).reshape(lead + (BRANCH_W,))
    return jnp.moveaxis(a, -1, axis)


def _sb_body(tq, tk, q_ref, k_ref, v_ref, qg_ref, kg_ref, u_ref, o_ref,
             kn_scr, vb_scr, qn_scr, hl_scr, lsig_scr, logit_scr, run_scr, acc_scr):
    qi = pl.program_id(2)
    nsub = tq // tk
    nblk = kn_scr.shape[0]
    lane = lax.broadcasted_iota(jnp.int32, (1, LANES), 1)
    first = lane < HEAD_DIM
    m_a = first.astype(F32)
    m_b = 1.0 - m_a

    def headnorm(x, g):
        x2 = x * x
        s_a = jnp.sum(x2 * m_a, axis=-1, keepdims=True)
        s_b = jnp.sum(x2 * m_b, axis=-1, keepdims=True)
        ms = jnp.where(first, s_a, s_b) * (1.0 / HEAD_DIM)
        return x * lax.rsqrt(ms + NORM_EPS) * g

    @pl.when(qi == 0)
    def _():
        kn = headnorm(k_ref[0].astype(F32), kg_ref[...])
        v = v_ref[0].astype(F32)
        kn_scr[:, :tk, :] = (kn * m_a).astype(BF16).reshape(nblk, tk, LANES)
        kn_scr[:, tk:, :] = (kn * m_b).astype(BF16).reshape(nblk, tk, LANES)
        vb_scr[:, :tk, :] = (v * m_a).astype(BF16).reshape(nblk, tk, LANES)
        vb_scr[:, tk:, :] = (v * m_b).astype(BF16).reshape(nblk, tk, LANES)

    qn_scr[...] = (headnorm(q_ref[0].astype(F32), qg_ref[...]) * (LOG2E * HEAD_DIM ** -0.5)).astype(BF16)
    run_scr[...] = jnp.zeros_like(run_scr)
    acc_scr[...] = jnp.zeros_like(acc_scr)
    uu = u_ref[...]
    rowi = lax.broadcasted_iota(jnp.int32, (SB_ROWS, 2 * tk), 0)
    coli = lax.broadcasted_iota(jnp.int32, (SB_ROWS, 2 * tk), 1) & (tk - 1)

    last = qi * nsub + (nsub - 1)
    heads = (slice(0, tk), slice(tk, 2 * tk))
    groups = [slice(s * SB_ROWS, (s + 1) * SB_ROWS) for s in range(tq // SB_ROWS)]

    def score(m, masked, rows):
        z = lax.dot_general(qn_scr[rows, :], kn_scr[jnp.maximum(last - m, 0)], NT_DIMS,
                            preferred_element_type=F32)
        l1m = -(jnp.maximum(z, 0.0) + jnp.log2(1.0 + jnp.exp2(-jnp.abs(z))))
        lsig = z + l1m
        if masked:
            mask = (coli + ((nsub - 1 - m) * tk - rows.start)) < rowi
            l1m = jnp.where(mask, l1m, 0.0)
            lsig = jnp.where(mask, lsig, MASKED_LOGIT)
        hi = l1m.astype(BF16)
        lo = (l1m - hi.astype(F32)).astype(BF16)
        for h, sl in enumerate(heads):
            hl_scr[h, rows, :] = jnp.concatenate([hi[:, sl], lo[:, sl]], axis=1)
        lsig_scr[rows, :] = lsig

    def suffix(rows):
        for h, sl in enumerate(heads):
            ce = _dot(hl_scr[h, rows, :], uu)
            run = run_scr[h, rows, :]
            logit_scr[rows, sl] = lsig_scr[rows, sl] + ce[:, :tk] + run
            run_scr[h, rows, :] = run + ce[:, tk:]

    def weigh(m, rows):
        acc_scr[rows, :] += _dot(jnp.exp2(logit_scr[rows, :]).astype(BF16), vb_scr[last - m])

    def body(m, carry):
        for rows in groups:
            weigh(m, rows)
            suffix(rows)
            score(m + 2, False, rows)
        return carry

    def live(rows, m):
        return (nsub - 1 - m) * tk <= rows.stop - 2

    def cut(rows, m):
        return (nsub - m) * tk - 1 >= rows.start

    for rows in groups:
        if live(rows, 0):
            score(0, cut(rows, 0), rows)
    for rows in groups:
        if live(rows, 0):
            suffix(rows)
        if live(rows, 1):
            score(1, cut(rows, 1), rows)
    for m in range(nsub):
        for rows in groups:
            if live(rows, m):
                weigh(m, rows)
            if live(rows, m + 1):
                suffix(rows)
            if live(rows, m + 2):
                score(m + 2, cut(rows, m + 2), rows)
    lax.fori_loop(nsub, (qi + 1) * nsub, body, 0)
    o_ref[0] = acc_scr[...]


def _sb_attention(P, qg, kg, uu, tq=1024, tk=LANES):
    b, t, _ = P.shape
    tq = min(tq, t)
    npair = N_HEADS // 2
    qoff, koff, voff = 0, BRANCH_W // LANES, 2 * BRANCH_W // LANES
    stacked = pltpu.VMEM((t // tk, 2 * tk, LANES), BF16)
    return pl.pallas_call(
        functools.partial(_sb_body, tq, tk),
        grid=(b, npair, t // tq),
        in_specs=[
            pl.BlockSpec((1, tq, LANES), lambda i, h, j: (i, j, qoff + h)),
            pl.BlockSpec((1, t, LANES), lambda i, h, j: (i, 0, koff + h)),
            pl.BlockSpec((1, t, LANES), lambda i, h, j: (i, 0, voff + h)),
            pl.BlockSpec((1, LANES), lambda i, h, j: (0, 0)),
            pl.BlockSpec((1, LANES), lambda i, h, j: (0, 0)),
            pl.BlockSpec((2 * tk, 2 * tk), lambda i, h, j: (0, 0)),
        ],
        out_specs=pl.BlockSpec((1, tq, LANES), lambda i, h, j: (i, j, h)),
        out_shape=jax.ShapeDtypeStruct((b, t, BRANCH_W), F32),
        scratch_shapes=[stacked, stacked,
                        pltpu.VMEM((tq, LANES), BF16),
                        pltpu.VMEM((2, tq, 2 * tk), BF16),
                        pltpu.VMEM((tq, 2 * tk), F32),
                        pltpu.VMEM((tq, 2 * tk), F32),
                        pltpu.VMEM((2, tq, LANES), F32),
                        pltpu.VMEM((tq, LANES), F32)],
        compiler_params=_params("parallel", "parallel", "arbitrary"),
        cost_estimate=pl.CostEstimate(flops=b * N_HEADS * t * t * (2 * HEAD_DIM + 4 * tk + 2 * HEAD_DIM) // 2,
                                      transcendentals=3 * b * N_HEADS * t * t // 2,
                                      bytes_accessed=2 * 3 * b * t * BRANCH_W + 4 * b * t * BRANCH_W),
        name="sb_attn",
    )(P, P, P, qg, kg, uu)


def _memkv_body(m_ref, g_ref, w_ref, kg_ref, k_o, v_o):
    kv = _dot(_rms(m_ref[0], g_ref[...]).astype(BF16), w_ref[...])
    for h in range(MEM_HEADS):
        sl = slice(h * MEM_HD, (h + 1) * MEM_HD)
        k_o[0, :, sl] = _rms(kv[:, sl], kg_ref[...]).astype(BF16)
    v_o[0] = kv[:, BRANCH_W:].astype(BF16)


def _mem_kv(mem, g, w_bf16, kg):
    b = mem.shape[0]
    out = jax.ShapeDtypeStruct((b, MEM_LEN, BRANCH_W), BF16)
    blk = pl.BlockSpec((1, MEM_LEN, BRANCH_W), lambda i: (i, 0, 0))
    return pl.pallas_call(
        _memkv_body,
        grid=(b,),
        in_specs=[
            pl.BlockSpec((1, MEM_LEN, D_MODEL), lambda i: (i, 0, 0)),
            pl.BlockSpec((1, D_MODEL), lambda i: (0, 0)),
            pl.BlockSpec((D_MODEL, 2 * BRANCH_W), lambda i: (0, 0)),
            pl.BlockSpec((1, MEM_HD), lambda i: (0, 0)),
        ],
        out_specs=[blk, blk],
        out_shape=[out, out],
        compiler_params=_params("parallel"),
        name="mem_kv",
    )(mem, g, w_bf16, kg)


def _memattn_body(q_ref, k_ref, v_ref, qg_ref, o_ref):
    q = q_ref[0].astype(F32)
    for h in range(MEM_HEADS):
        sl = slice(h * MEM_HD, (h + 1) * MEM_HD)
        qh = (_rms(q[:, sl], qg_ref[...]) * (MEM_HD ** -0.5)).astype(BF16)
        s = lax.dot_general(qh, k_ref[0, :, sl], NT_DIMS, preferred_element_type=F32)
        p = jnp.exp(s - jnp.max(s, axis=-1, keepdims=True))
        den = jnp.sum(p, axis=-1, keepdims=True)
        o_ref[0, :, sl] = _dot(p.astype(BF16), v_ref[0, :, sl]) / den


def _mem_attention(P, kn, mv, qg, tq=512):
    b, t, _ = P.shape
    qoff = MEM_Q_OFF // BRANCH_W
    kv = pl.BlockSpec((1, MEM_LEN, BRANCH_W), lambda i, j: (i, 0, 0))
    return pl.pallas_call(
        _memattn_body,
        grid=(b, t // tq),
        in_specs=[
            pl.BlockSpec((1, tq, BRANCH_W), lambda i, j: (i, j, qoff)),
            kv, kv,
            pl.BlockSpec((1, MEM_HD), lambda i, j: (0, 0)),
        ],
        out_specs=pl.BlockSpec((1, tq, BRANCH_W), lambda i, j: (i, j, 0)),
        out_shape=jax.ShapeDtypeStruct((b, t, BRANCH_W), F32),
        compiler_params=_params("parallel", "parallel"),
        name="mem_attn",
    )(P, kn, mv, qg)


def _merge_body(x_ref, y_ref, bonus_ref, g_ref, yb_ref, ym_ref, g0_ref, g1_ref, g2_ref,
                lng_ref, lnb_ref, bd_ref, wbr_ref, wout_ref, o_ref):
    bd = bd_ref[...]
    y = y_ref[0].T
    d = y - _dot_ones(y, bd) * (1.0 / HEAD_DIM)
    var = _dot_ones(d * d, bd) * (1.0 / HEAD_DIM)
    ya = (d * lax.rsqrt(var + LNX_EPS) * lng_ref[...] + lnb_ref[...] + bonus_ref[0].astype(F32)) * g_ref[0].astype(F32)
    m = (_sigmoid(g0_ref[0].astype(F32)) * _dot(ya.astype(BF16), wbr_ref[0])
         + _sigmoid(g1_ref[0].astype(F32)) * _dot(yb_ref[0].astype(BF16), wbr_ref[1])
         + _sigmoid(g2_ref[0].astype(F32)) * _dot(ym_ref[0].astype(BF16), wbr_ref[2]))
    o_ref[0] = x_ref[0] + _dot(m.astype(BF16), wout_ref[...])


def _merge(x, y, bonus, g, yb, ym, gates, lng, lnb, bd, wbr, wout, tq=512):
    b, t, _ = x.shape
    tok = pl.BlockSpec((1, tq, BRANCH_W), lambda i, j: (i, j, 0))
    xblk = pl.BlockSpec((1, tq, D_MODEL), lambda i, j: (i, j, 0))
    goff = GATE_OFF // D_MODEL

    def gate(n):
        return pl.BlockSpec((1, tq, D_MODEL), lambda i, j: (i, j, goff + n))

    def full(a):
        return pl.BlockSpec(a.shape, lambda i, j: (0,) * a.ndim)

    return pl.pallas_call(
        _merge_body,
        grid=(b, t // tq),
        in_specs=[xblk, pl.BlockSpec((1, BRANCH_W, tq), lambda i, j: (i, 0, j)), tok, tok, tok, tok,
                  gate(0), gate(1), gate(2),
                  full(lng), full(lnb), full(bd), full(wbr), full(wout)],
        out_specs=xblk,
        out_shape=jax.ShapeDtypeStruct(x.shape, F32),
        compiler_params=_params("parallel", "parallel"),
        name="merge",
    )(x, y, bonus, g, yb, ym, gates, gates, gates, lng, lnb, bd, wbr, wout)


FF_TILE = 1408


def _ffn_body(tq, x_ref, g_ref, wup_ref, cw_ref, cb_ref, wdn_ref, o_ref, carry):
    @pl.when(pl.program_id(1) == 0)
    def _():
        carry[...] = jnp.zeros_like(carry)

    x = x_ref[0]
    h = _rms(x, g_ref[...]).astype(BF16)
    row = lax.broadcasted_iota(jnp.int32, (tq, 1), 0)
    acc = jnp.zeros((tq, D_MODEL), F32)
    for f in range(D_FF // FF_TILE):
        halves = []
        for half in range(2):
            c0 = half * D_FF + f * FF_TILE
            sl = slice(c0, c0 + FF_TILE)
            hh = _dot(h, wup_ref[:, sl])
            cp = carry[:, sl]
            s1 = jnp.where(row == 0, cp[7:8], pltpu.roll(hh, 1, 0))
            s2 = jnp.where(row == 0, cp[6:7], jnp.where(row == 1, cp[7:8], pltpu.roll(hh, 2, 0)))
            carry[:, sl] = hh[tq - 8:, :]
            cw = cw_ref[:, sl]
            halves.append(cb_ref[:, sl] + s2 * cw[0:1] + s1 * cw[1:2] + hh * cw[2:3])
        gate, val = halves
        u = gate * _sigmoid(gate) * val
        acc = acc + _dot(u.astype(BF16), wdn_ref[f * FF_TILE:(f + 1) * FF_TILE, :])
    o_ref[0] = x + acc


def _ffn(x, g, wup, cw, cb, wdn, tq=512):
    b, t, _ = x.shape
    xblk = pl.BlockSpec((1, tq, D_MODEL), lambda i, j: (i, j, 0))

    def full(a):
        return pl.BlockSpec(a.shape, lambda i, j: (0,) * a.ndim, pipeline_mode=pl.Buffered(1))

    return pl.pallas_call(
        functools.partial(_ffn_body, tq),
        grid=(b, t // tq),
        in_specs=[xblk, full(g), full(wup), full(cw), full(cb), full(wdn)],
        out_specs=xblk,
        out_shape=jax.ShapeDtypeStruct(x.shape, F32),
        scratch_shapes=[pltpu.VMEM((8, 2 * D_FF), F32)],
        compiler_params=_params("parallel", "arbitrary"),
        name="conv_ffn",
    )(x, g, wup, cw, cb, wdn)


def _row(v):
    return v.reshape(1, -1)


def kernel(x, mem, norm1_g, w_in, shift_mu, decay_w0, decay_w2, iclr_a0, iclr_a2, gate_g2, k_k, k_a, r_k, lnx_g, lnx_b, vres_v0, vres_v1, vres_v2, sb_q_norm_g, sb_k_norm_g, mem_norm_g, w_mem_kv, mem_q_norm_g, mem_k_norm_g, w_branch, w_out, norm2_g, w_up, conv_w, conv_b, w_down):
    b, t, _ = x.shape
    assert 2 * b * N_HEADS == LANES, "recurrence layout packs (key-half, batch, head) onto the lanes"
    depth = w_in.shape[0]

    head_id = jnp.arange(BRANCH_W) % N_HEADS
    bd = (head_id[:, None] == head_id[None, :]).astype(BF16)
    tri = jnp.arange(LANES)
    u_ext = jnp.concatenate([(tri[:, None] > tri[None, :]).astype(BF16),
                             jnp.ones((LANES, LANES), BF16)], axis=1)
    u_ext = jnp.concatenate([u_ext, u_ext], axis=0)
    zpad64 = jnp.zeros((64, BRANCH_W), F32)

    def rkv_order(a):
        head = _rec_order(a[..., :3 * BRANCH_W].reshape(a.shape[:-1] + (3, BRANCH_W)))
        return jnp.concatenate([head.reshape(a.shape[:-1] + (3 * BRANCH_W,)), a[..., 3 * BRANCH_W:]], axis=-1)

    w_rwkv_all = jnp.pad(rkv_order(w_in[..., :RWKV_COLS]), ((0, 0), (0, 0), (0, RWKV_PAD - RWKV_COLS)))
    w_branch_r = w_branch.at[:, 0].set(_rec_order(w_branch[:, 0], axis=1)).reshape(depth, -1, D_MODEL)

    v_first = None
    for l in range(depth):
        x2d = x.reshape(b * t, D_MODEL)
        P, Pr = _proj(x2d, _row(norm1_g[l]), _to_bf16(w_rwkv_all, l), _to_bf16(w_in, l, RWKV_COLS))
        P, Pr = P.reshape(b, t, RWKV_PAD), Pr.reshape(b, t, REST_COLS)

        vecs = (_row(rkv_order(shift_mu[l])), _row(_rec_order(decay_w0[l])), _row(_rec_order(iclr_a0[l])),
                _row(_rec_order(k_k[l])), _row(_rec_order(k_a[l])), _row(_rec_order(r_k[l].reshape(-1))))
        mats = (jnp.concatenate([_rec_order(decay_w2[l]), zpad64], axis=0).astype(BF16),
                jnp.concatenate([zpad64, _rec_order(iclr_a2[l])], axis=0).astype(BF16),
                _rec_order(gate_g2[l]).astype(BF16))
        vres = None
        if l > 0:
            v1 = jnp.pad(_rec_order(vres_v1[l - 1], axis=0), ((0, 0), (0, LANES - vres_v1.shape[-1]))).astype(BF16)
            v2 = jnp.pad(_rec_order(vres_v2[l - 1]), ((0, LANES - vres_v2.shape[-2]), (0, 0))).astype(BF16)
            vres = (v_first, _row(_rec_order(vres_v0[l - 1])), v1, v2)
        r, dec, k2, v, nkk, bb, g, bonus, *v_plain = _prep(P, vecs, mats, bd, vres)
        if l == 0:
            v_first = v_plain[0]

        kn, mv = _mem_kv(mem, _row(mem_norm_g[l]), _to_bf16(w_mem_kv, l), _row(mem_k_norm_g[l]))
        ym = _mem_attention(Pr, kn, mv, _row(mem_q_norm_g[l]))

        lanes = lambda a: a.reshape(HALF_J, LANES, t)
        y = _recurrence(lanes(dec), lanes(nkk), lanes(bb), lanes(k2), lanes(r), _value_lanes(v), ym)
        y = _from_value_lanes(y, b)

        qg2 = _row(jnp.concatenate([sb_q_norm_g[l], sb_q_norm_g[l]]))
        kg2 = _row(jnp.concatenate([sb_k_norm_g[l], sb_k_norm_g[l]]))
        yb = _sb_attention(Pr, qg2, kg2, u_ext)

        wbr = _to_bf16(w_branch_r, l).reshape(w_branch.shape[1:])
        x = _merge(x, y, bonus, g, yb, ym, Pr, _row(_rec_order(lnx_g[l])), _row(_rec_order(lnx_b[l])), bd, wbr,
                   _to_bf16(w_out, l))
        x = _ffn(x, _row(norm2_g[l]), _to_bf16(w_up, l), conv_w[l], _row(conv_b[l]), _to_bf16(w_down, l))
    return x
```

```python
import functools

import jax
import jax.numpy as jnp
from jax import lax
from jax.experimental import pallas as pl
from jax.experimental.pallas import tpu as pltpu

F32 = jnp.float32
BF16 = jnp.bfloat16

D_MODEL = 1024
N_HEADS = 8
HEAD_DIM = 64
BRANCH_W = 512
RWKV_COLS = 1792
RWKV_PAD = 2048
REST_COLS = 5120
MEM_Q_OFF = 1536
GATE_OFF = 2048
LOG2E = 1.4426950408889634
MEM_LEN = 256
MEM_HEADS = 4
MEM_HD = 128
D_FF = 2816
NORM_EPS = 1e-6
LNX_EPS = 64e-5
LANES = 128
MASKED_LOGIT = -1e30
SB_ROWS = 128
VMEM_LIMIT = 56 * 1024 * 1024

NT_DIMS = (((1,), (1,)), ((), ()))


def _rms(xf, g, eps=NORM_EPS):
    return xf * lax.rsqrt(jnp.mean(xf * xf, axis=-1, keepdims=True) + eps) * g


def _sigmoid(x):
    return 1.0 / (1.0 + jnp.exp(-x))


def _softplus(x):
    return jnp.maximum(x, 0.0) + jnp.log(1.0 + jnp.exp(-jnp.abs(x)))


def _dot(a, b):
    return jnp.dot(a, b, preferred_element_type=F32)


def _split3(x):
    h1 = x.astype(BF16)
    r1 = x - h1.astype(F32)
    h2 = r1.astype(BF16)
    h3 = (r1 - h2.astype(F32)).astype(BF16)
    return h1, h2, h3


def _dot_ones(x, ones_bf16):
    h1, h2, h3 = _split3(x)
    return _dot(h1, ones_bf16) + _dot(h2, ones_bf16) + _dot(h3, ones_bf16)


def _dot_ones2(x, ones_bf16):
    h1, h2, _ = _split3(x)
    return _dot(h1, ones_bf16) + _dot(h2, ones_bf16)


def _params(*sem):
    return pltpu.CompilerParams(dimension_semantics=sem, vmem_limit_bytes=VMEM_LIMIT)


def _cast_body(w_ref, o_ref):
    o_ref[...] = w_ref[...].astype(BF16)


def _to_bf16(w, l, col0=0, ncols=None, out_cols=None):
    _, rows, total = w.shape
    ncols = total - col0 if ncols is None else ncols
    out_cols = ncols if out_cols is None else out_cols
    blk = 2 * LANES
    while all(n % (2 * blk) == 0 for n in (col0, ncols, out_cols)) and blk < 512:
        blk *= 2
    nreal = ncols // blk

    def body(w_ref, o_ref):
        @pl.when(pl.program_id(0) < nreal)
        def _():
            o_ref[...] = w_ref[...].astype(BF16)

        @pl.when(pl.program_id(0) >= nreal)
        def _():
            o_ref[...] = jnp.zeros_like(o_ref)

    return pl.pallas_call(
        body if out_cols > ncols else _cast_body,
        grid=(out_cols // blk,),
        in_specs=[pl.BlockSpec((None, rows, blk), lambda j: (l, 0, col0 // blk + jnp.minimum(j, nreal - 1)))],
        out_specs=pl.BlockSpec((rows, blk), lambda j: (0, j)),
        out_shape=jax.ShapeDtypeStruct((rows, out_cols), BF16),
        compiler_params=_params("parallel"),
        name="to_bf16",
    )(w)


def _proj_body(x_ref, g_ref, wa_ref, wb_ref, oa_ref, ob_ref):
    h = _rms(x_ref[...], g_ref[...]).astype(BF16)
    oa_ref[...] = _dot(h, wa_ref[...])
    ob_ref[...] = _dot(h, wb_ref[...]).astype(BF16)


def _proj(x2d, g, w_rwkv, w_rest, tq=512):
    n = x2d.shape[0]
    ca, cb = w_rwkv.shape[1], w_rest.shape[1]

    def full(a):
        return pl.BlockSpec(a.shape, lambda i: (0, 0), pipeline_mode=pl.Buffered(1))

    return pl.pallas_call(
        _proj_body,
        grid=(n // tq,),
        in_specs=[pl.BlockSpec((tq, D_MODEL), lambda i: (i, 0)), full(g), full(w_rwkv), full(w_rest)],
        out_specs=[pl.BlockSpec((tq, ca), lambda i: (i, 0)), pl.BlockSpec((tq, cb), lambda i: (i, 0))],
        out_shape=[jax.ShapeDtypeStruct((n, ca), F32), jax.ShapeDtypeStruct((n, cb), BF16)],
        compiler_params=_params("parallel"),
        cost_estimate=pl.CostEstimate(flops=2 * n * D_MODEL * (ca + cb), transcendentals=n,
                                      bytes_accessed=4 * n * D_MODEL + 2 * D_MODEL * (ca + cb) + n * (4 * ca + 2 * cb)),
        name="proj",
    )(x2d, g, w_rwkv, w_rest)


def _prep_body(has_vres, tq, *refs):
    if has_vres:
        (p_ref, mu_ref, w0_ref, w2_ref, a0_ref, a2_ref, g2_ref, kk_ref, ka_ref, rk_ref, bd_ref,
         vf_ref, v0_ref, v1_ref, v2_ref,
         r_o, dec_o, k_o, v_o, nkk_o, bb_o, g_o, bonus_o, carry) = refs
        vfirst_o = None
    else:
        (p_ref, mu_ref, w0_ref, w2_ref, a0_ref, a2_ref, g2_ref, kk_ref, ka_ref, rk_ref, bd_ref,
         r_o, dec_o, k_o, v_o, nkk_o, bb_o, g_o, bonus_o, vfirst_o, carry) = refs

    @pl.when(pl.program_id(1) == 0)
    def _():
        carry[...] = jnp.zeros_like(carry)

    p = p_ref[0]
    row = lax.broadcasted_iota(jnp.int32, (tq, 1), 0)
    prev = jnp.where(row == 0, carry[7:8, :], pltpu.roll(p, 1, 0))
    carry[...] = p[tq - 8:, :]
    ps = p + (prev - p) * mu_ref[...]

    r = ps[:, 0:512]
    k = ps[:, 512:1024]
    v = ps[:, 1024:1536]
    wa = ps[:, 1536:1664]
    gl = ps[:, 1664:1792]

    wlog = -_softplus(-(w0_ref[...] + _dot(jnp.tanh(wa).astype(BF16), w2_ref[...]))) - 0.5
    dec = jnp.exp(-jnp.exp(wlog))
    a = _sigmoid(a0_ref[...] + _dot(wa.astype(BF16), a2_ref[...]))
    g = _dot(_sigmoid(gl).astype(BF16), g2_ref[...])
    if has_vres:
        lo = _dot(v.astype(BF16), v1_ref[...])
        v = v + (vf_ref[0] - v) * _sigmoid(v0_ref[...] + _dot(lo.astype(BF16), v2_ref[...]))
    bd = bd_ref[...]
    kk = k * kk_ref[...]
    kk = kk / jnp.maximum(jnp.sqrt(_dot_ones(kk * kk, bd)), 1e-12)
    k2 = k * (1.0 + (a - 1.0) * ka_ref[...])
    bonus = _dot_ones(r * k2 * rk_ref[...], bd) * v

    def rec_layout(x):
        return x.T.reshape(HALF_J, 2, 1, N_HEADS, tq)

    r_o[...] = rec_layout(r)
    dec_o[...] = rec_layout(dec)
    k_o[...] = rec_layout(k2)
    v_o[0] = v.T
    nkk_o[...] = rec_layout(-kk)
    bb_o[...] = rec_layout(kk * a)
    g_o[0] = g.astype(BF16)
    bonus_o[0] = bonus.astype(BF16)
    if vfirst_o is not None:
        vfirst_o[0] = v


def _prep(P, vecs, mats, bd, vres, tq=256):
    b, t, _ = P.shape
    has_vres = vres is not None
    tok = pl.BlockSpec((1, tq, BRANCH_W), lambda i, j: (i, j, 0))

    def full(a):
        return pl.BlockSpec(a.shape, lambda i, j: (0,) * a.ndim)

    mu, w0, a0, kk, ka, rk = vecs
    w2, a2, g2 = mats
    args = [P, mu, w0, w2, a0, a2, g2, kk, ka, rk, bd]
    in_specs = [pl.BlockSpec((1, tq, RWKV_COLS), lambda i, j: (i, j, 0))] + [full(a) for a in args[1:]]
    if has_vres:
        vf, v0, v1, v2 = vres
        args += [vf, v0, v1, v2]
        in_specs += [tok, full(v0), full(v1), full(v2)]
    out = jax.ShapeDtypeStruct((b, t, BRANCH_W), F32)
    out_t = jax.ShapeDtypeStruct((HALF_J, 2, b, N_HEADS, t), F32)
    tok_t = pl.BlockSpec((HALF_J, 2, 1, N_HEADS, tq), lambda i, j: (0, 0, i, 0, j))
    out_v = jax.ShapeDtypeStruct((b, BRANCH_W, t), F32)
    tok_v = pl.BlockSpec((1, BRANCH_W, tq), lambda i, j: (i, 0, j))
    out_h = jax.ShapeDtypeStruct((b, t, BRANCH_W), BF16)
    return pl.pallas_call(
        functools.partial(_prep_body, has_vres, tq),
        grid=(b, t // tq),
        in_specs=in_specs,
        out_specs=[tok_t] * 3 + [tok_v] + [tok_t] * 2 + [tok] * (2 if has_vres else 3),
        out_shape=[out_t] * 3 + [out_v] + [out_t] * 2 + [out_h] * 2 + ([] if has_vres else [out]),
        scratch_shapes=[pltpu.VMEM((8, RWKV_COLS), F32)],
        compiler_params=_params("parallel", "arbitrary"),
        name="rwkv_prep",
    )(*args)


HALF_J = HEAD_DIM // 2
SUBLANES = 8


def _rec_body(tc, w_ref, nk_ref, b_ref, k_ref, r_ref, v_ref, after_ref, y_ref,
              s_scr, w_scr, nk_scr, b_scr, k_scr, r_scr, ypart_scr):
    del after_ref

    @pl.when(pl.program_id(0) == 0)
    def _():
        s_scr[...] = jnp.zeros_like(s_scr)

    for src, dst in ((w_ref, w_scr), (nk_ref, nk_scr), (b_ref, b_scr), (k_ref, k_scr), (r_ref, r_scr)):
        for j in range(HALF_J):
            dst[j] = src[j].T

    groups = [pl.ds(g * SUBLANES, SUBLANES) for g in range(HEAD_DIM // SUBLANES)]

    def fold(x):
        return x + pltpu.roll(x, HEAD_DIM, 1)

    def step(t, carry):
        part = [None] * len(groups)
        for j in range(HALF_J):
            nk = nk_scr[j, pl.ds(t, 1), :]
            for g, rows in enumerate(groups):
                term = s_scr[j, rows, :] * nk
                part[g] = term if part[g] is None else part[g] + term
        sa = [fold(p) for p in part]
        v = [jnp.concatenate([v_ref[t, rows, :]] * 2, axis=-1) for rows in groups]
        yacc = [None] * len(groups)
        for j in range(HALF_J):
            w = w_scr[j, pl.ds(t, 1), :]
            bb = b_scr[j, pl.ds(t, 1), :]
            kk = k_scr[j, pl.ds(t, 1), :]
            rr = r_scr[j, pl.ds(t, 1), :]
            for g, rows in enumerate(groups):
                sn = s_scr[j, rows, :] * w + (sa[g] * bb + v[g] * kk)
                s_scr[j, rows, :] = sn
                term = sn * rr
                yacc[g] = term if yacc[g] is None else yacc[g] + term
        for g, rows in enumerate(groups):
            ypart_scr[t, rows, :] = yacc[g]
        return carry

    lax.fori_loop(0, tc, step, 0)
    y = fold(ypart_scr[...].reshape(tc * HEAD_DIM, LANES))
    y_ref[...] = y[:, :HEAD_DIM].reshape(tc, HEAD_DIM, HEAD_DIM)


def _recurrence(dec, nkk, bb, k2, r, v, after, tc=128):
    t = dec.shape[-1]
    kspec = pl.BlockSpec((HALF_J, LANES, tc), lambda i: (0, 0, i))
    vspec = pl.BlockSpec((tc, HEAD_DIM, HEAD_DIM), lambda i: (i, 0, 0))
    turned = pltpu.VMEM((HALF_J, tc, LANES), F32)
    return pl.pallas_call(
        functools.partial(_rec_body, tc),
        grid=(t // tc,),
        in_specs=[kspec] * 5 + [vspec, pl.BlockSpec(memory_space=pl.ANY)],
        out_specs=vspec,
        out_shape=jax.ShapeDtypeStruct((t, HEAD_DIM, HEAD_DIM), F32),
        scratch_shapes=[pltpu.VMEM((HALF_J, HEAD_DIM, LANES), F32)] + [turned] * 5
                       + [pltpu.VMEM((tc, HEAD_DIM, LANES), F32)],
        compiler_params=_params("arbitrary"),
        cost_estimate=pl.CostEstimate(flops=9 * t * HALF_J * HEAD_DIM * LANES, transcendentals=0,
                                      bytes_accessed=4 * t * (5 * HALF_J * LANES + 2 * HEAD_DIM * HEAD_DIM)),
        name="rwkv_rec",
    )(dec, nkk, bb, k2, r, v, after)


def _value_lanes(v_t):
    b, _, t = v_t.shape
    return v_t.reshape(b, HALF_J, 2, N_HEADS, t).transpose(4, 2, 1, 0, 3).reshape(t, HEAD_DIM, b * N_HEADS)


def _from_value_lanes(y, b):
    t = y.shape[0]
    return y.reshape(t, 2, HALF_J, b, N_HEADS).transpose(3, 2, 1, 4, 0).reshape(b, BRANCH_W, t)


def _rec_order(a, axis=-1):
    a = jnp.moveaxis(a, axis, -1)
    lead = a.shape[:-1]
    a = jnp.swapaxes(a.reshape(lead + (N_HEADS, 2, HALF_J)), -1, -3).reshape(lead + (BRANCH_W,))
    return jnp.moveaxis(a, -1, axis)


def _sb_body(tq, tk, q_ref, k_ref, v_ref, qg_ref, kg_ref, u_ref, o_ref,
             kn_scr, vb_scr, qn_scr, hl_scr, lsig_scr, logit_scr, run_scr, acc_scr):
    qi = pl.program_id(2)
    nsub = tq // tk
    nblk = kn_scr.shape[0]
    lane = lax.broadcasted_iota(jnp.int32, (1, LANES), 1)
    first = lane < HEAD_DIM
    m_a = first.astype(F32)
    m_b = 1.0 - m_a

    def headnorm(x, g):
        x2 = x * x
        s_a = jnp.sum(x2 * m_a, axis=-1, keepdims=True)
        s_b = jnp.sum(x2 * m_b, axis=-1, keepdims=True)
        ms = jnp.where(first, s_a, s_b) * (1.0 / HEAD_DIM)
        return x * lax.rsqrt(ms + NORM_EPS) * g

    @pl.when(qi == 0)
    def _():
        kn = headnorm(k_ref[0].astype(F32), kg_ref[...])
        v = v_ref[0].astype(F32)
        kn_scr[:, :tk, :] = (kn * m_a).astype(BF16).reshape(nblk, tk, LANES)
        kn_scr[:, tk:, :] = (kn * m_b).astype(BF16).reshape(nblk, tk, LANES)
        vb_scr[:, :tk, :] = (v * m_a).astype(BF16).reshape(nblk, tk, LANES)
        vb_scr[:, tk:, :] = (v * m_b).astype(BF16).reshape(nblk, tk, LANES)

    qn_scr[...] = (headnorm(q_ref[0].astype(F32), qg_ref[...]) * (LOG2E * HEAD_DIM ** -0.5)).astype(BF16)
    run_scr[...] = jnp.zeros_like(run_scr)
    acc_scr[...] = jnp.zeros_like(acc_scr)
    uu = u_ref[...]
    rowi = lax.broadcasted_iota(jnp.int32, (SB_ROWS, 2 * tk), 0)
    coli = lax.broadcasted_iota(jnp.int32, (SB_ROWS, 2 * tk), 1) & (tk - 1)

    last = qi * nsub + (nsub - 1)
    heads = (slice(0, tk), slice(tk, 2 * tk))
    groups = [slice(s * SB_ROWS, (s + 1) * SB_ROWS) for s in range(tq // SB_ROWS)]

    def score(m, masked, rows):
        z = lax.dot_general(qn_scr[rows, :], kn_scr[jnp.maximum(last - m, 0)], NT_DIMS,
                            preferred_element_type=F32)
        l1m = -(jnp.maximum(z, 0.0) + jnp.log2(1.0 + jnp.exp2(-jnp.abs(z))))
        lsig = z + l1m
        if masked:
            mask = (coli + ((nsub - 1 - m) * tk - rows.start)) < rowi
            l1m = jnp.where(mask, l1m, 0.0)
            lsig = jnp.where(mask, lsig, MASKED_LOGIT)
        hi = l1m.astype(BF16)
        lo = (l1m - hi.astype(F32)).astype(BF16)
        for h, sl in enumerate(heads):
            hl_scr[h, rows, :] = jnp.concatenate([hi[:, sl], lo[:, sl]], axis=1)
        lsig_scr[rows, :] = lsig

    def suffix(rows):
        for h, sl in enumerate(heads):
            ce = _dot(hl_scr[h, rows, :], uu)
            run = run_scr[h, rows, :]
            logit_scr[rows, sl] = lsig_scr[rows, sl] + ce[:, :tk] + run
            run_scr[h, rows, :] = run + ce[:, tk:]

    def weigh(m, rows):
        acc_scr[rows, :] += _dot(jnp.exp2(logit_scr[rows, :]).astype(BF16), vb_scr[last - m])

    def body(m, carry):
        for rows in groups:
            weigh(m, rows)
            suffix(rows)
            score(m + 2, False, rows)
        return carry

    def live(rows, m):
        return (nsub - 1 - m) * tk <= rows.stop - 2

    def cut(rows, m):
        return (nsub - m) * tk - 1 >= rows.start

    for rows in groups:
        if live(rows, 0):
            score(0, cut(rows, 0), rows)
    for rows in groups:
        if live(rows, 0):
            suffix(rows)
        if live(rows, 1):
            score(1, cut(rows, 1), rows)
    for m in range(nsub):
        for rows in groups:
            if live(rows, m):
                weigh(m, rows)
            if live(rows, m + 1):
                suffix(rows)
            if live(rows, m + 2):
                score(m + 2, cut(rows, m + 2), rows)
    lax.fori_loop(nsub, (qi + 1) * nsub, body, 0)
    o_ref[0] = acc_scr[...]


def _sb_attention(P, qg, kg, uu, tq=1024, tk=LANES):
    b, t, _ = P.shape
    tq = min(tq, t)
    npair = N_HEADS // 2
    qoff, koff, voff = 0, BRANCH_W // LANES, 2 * BRANCH_W // LANES
    stacked = pltpu.VMEM((t // tk, 2 * tk, LANES), BF16)
    return pl.pallas_call(
        functools.partial(_sb_body, tq, tk),
        grid=(b, npair, t // tq),
        in_specs=[
            pl.BlockSpec((1, tq, LANES), lambda i, h, j: (i, j, qoff + h)),
            pl.BlockSpec((1, t, LANES), lambda i, h, j: (i, 0, koff + h)),
            pl.BlockSpec((1, t, LANES), lambda i, h, j: (i, 0, voff + h)),
            pl.BlockSpec((1, LANES), lambda i, h, j: (0, 0)),
            pl.BlockSpec((1, LANES), lambda i, h, j: (0, 0)),
            pl.BlockSpec((2 * tk, 2 * tk), lambda i, h, j: (0, 0)),
        ],
        out_specs=pl.BlockSpec((1, tq, LANES), lambda i, h, j: (i, j, h)),
        out_shape=jax.ShapeDtypeStruct((b, t, BRANCH_W), F32),
        scratch_shapes=[stacked, stacked,
                        pltpu.VMEM((tq, LANES), BF16),
                        pltpu.VMEM((2, tq, 2 * tk), BF16),
                        pltpu.VMEM((tq, 2 * tk), F32),
                        pltpu.VMEM((tq, 2 * tk), F32),
                        pltpu.VMEM((2, tq, LANES), F32),
                        pltpu.VMEM((tq, LANES), F32)],
        compiler_params=_params("parallel", "parallel", "arbitrary"),
        cost_estimate=pl.CostEstimate(flops=b * N_HEADS * t * t * (2 * HEAD_DIM + 4 * tk + 2 * HEAD_DIM) // 2,
                                      transcendentals=3 * b * N_HEADS * t * t // 2,
                                      bytes_accessed=2 * 3 * b * t * BRANCH_W + 4 * b * t * BRANCH_W),
        name="sb_attn",
    )(P, P, P, qg, kg, uu)


def _memkv_body(m_ref, g_ref, w_ref, kg_ref, k_o, v_o):
    kv = _dot(_rms(m_ref[0], g_ref[...]).astype(BF16), w_ref[...])
    for h in range(MEM_HEADS):
        sl = slice(h * MEM_HD, (h + 1) * MEM_HD)
        k_o[0, :, sl] = _rms(kv[:, sl], kg_ref[...]).astype(BF16)
    v_o[0] = kv[:, BRANCH_W:].astype(BF16)


def _mem_kv(mem, g, w_bf16, kg):
    b = mem.shape[0]
    out = jax.ShapeDtypeStruct((b, MEM_LEN, BRANCH_W), BF16)
    blk = pl.BlockSpec((1, MEM_LEN, BRANCH_W), lambda i: (i, 0, 0))
    return pl.pallas_call(
        _memkv_body,
        grid=(b,),
        in_specs=[
            pl.BlockSpec((1, MEM_LEN, D_MODEL), lambda i: (i, 0, 0)),
            pl.BlockSpec((1, D_MODEL), lambda i: (0, 0)),
            pl.BlockSpec((D_MODEL, 2 * BRANCH_W), lambda i: (0, 0)),
            pl.BlockSpec((1, MEM_HD), lambda i: (0, 0)),
        ],
        out_specs=[blk, blk],
        out_shape=[out, out],
        compiler_params=_params("parallel"),
        name="mem_kv",
    )(mem, g, w_bf16, kg)


def _memattn_body(q_ref, k_ref, v_ref, qg_ref, o_ref):
    q = q_ref[0].astype(F32)
    for h in range(MEM_HEADS):
        sl = slice(h * MEM_HD, (h + 1) * MEM_HD)
        qh = (_rms(q[:, sl], qg_ref[...]) * (MEM_HD ** -0.5)).astype(BF16)
        s = lax.dot_general(qh, k_ref[0, :, sl], NT_DIMS, preferred_element_type=F32)
        p = jnp.exp(s - jnp.max(s, axis=-1, keepdims=True))
        den = jnp.sum(p, axis=-1, keepdims=True)
        o_ref[0, :, sl] = _dot(p.astype(BF16), v_ref[0, :, sl]) / den


def _mem_attention(P, kn, mv, qg, tq=512):
    b, t, _ = P.shape
    qoff = MEM_Q_OFF // BRANCH_W
    kv = pl.BlockSpec((1, MEM_LEN, BRANCH_W), lambda i, j: (i, 0, 0))
    return pl.pallas_call(
        _memattn_body,
        grid=(b, t // tq),
        in_specs=[
            pl.BlockSpec((1, tq, BRANCH_W), lambda i, j: (i, j, qoff)),
            kv, kv,
            pl.BlockSpec((1, MEM_HD), lambda i, j: (0, 0)),
        ],
        out_specs=pl.BlockSpec((1, tq, BRANCH_W), lambda i, j: (i, j, 0)),
        out_shape=jax.ShapeDtypeStruct((b, t, BRANCH_W), F32),
        compiler_params=_params("parallel", "parallel"),
        name="mem_attn",
    )(P, kn, mv, qg)


def _merge_body(x_ref, y_ref, bonus_ref, g_ref, yb_ref, ym_ref, g0_ref, g1_ref, g2_ref,
                lng_ref, lnb_ref, bd_ref, wbr_ref, wout_ref, o_ref):
    bd = bd_ref[...]
    y = y_ref[0].T
    d = y - _dot_ones2(y, bd) * (1.0 / HEAD_DIM)
    var = _dot_ones2(d * d, bd) * (1.0 / HEAD_DIM)
    ya = (d * lax.rsqrt(var + LNX_EPS) * lng_ref[...] + lnb_ref[...] + bonus_ref[0].astype(F32)) * g_ref[0].astype(F32)
    m = (_sigmoid(g0_ref[0].astype(F32)) * _dot(ya.astype(BF16), wbr_ref[0])
         + _sigmoid(g1_ref[0].astype(F32)) * _dot(yb_ref[0].astype(BF16), wbr_ref[1])
         + _sigmoid(g2_ref[0].astype(F32)) * _dot(ym_ref[0].astype(BF16), wbr_ref[2]))
    o_ref[0] = x_ref[0] + _dot(m.astype(BF16), wout_ref[...])


def _merge(x, y, bonus, g, yb, ym, gates, lng, lnb, bd, wbr, wout, tq=512):
    b, t, _ = x.shape
    tok = pl.BlockSpec((1, tq, BRANCH_W), lambda i, j: (i, j, 0))
    xblk = pl.BlockSpec((1, tq, D_MODEL), lambda i, j: (i, j, 0))
    goff = GATE_OFF // D_MODEL

    def gate(n):
        return pl.BlockSpec((1, tq, D_MODEL), lambda i, j: (i, j, goff + n))

    def full(a):
        return pl.BlockSpec(a.shape, lambda i, j: (0,) * a.ndim)

    return pl.pallas_call(
        _merge_body,
        grid=(b, t // tq),
        in_specs=[xblk, pl.BlockSpec((1, BRANCH_W, tq), lambda i, j: (i, 0, j)), tok, tok, tok, tok,
                  gate(0), gate(1), gate(2),
                  full(lng), full(lnb), full(bd), full(wbr), full(wout)],
        out_specs=xblk,
        out_shape=jax.ShapeDtypeStruct(x.shape, F32),
        compiler_params=_params("parallel", "parallel"),
        name="merge",
    )(x, y, bonus, g, yb, ym, gates, gates, gates, lng, lnb, bd, wbr, wout)


FF_TILE = 1408


def _ffn_body(tq, x_ref, g_ref, wup_ref, cw_ref, cb_ref, wdn_ref, o_ref, carry):
    @pl.when(pl.program_id(1) == 0)
    def _():
        carry[...] = jnp.zeros_like(carry)

    x = x_ref[0]
    h = _rms(x, g_ref[...]).astype(BF16)
    row = lax.broadcasted_iota(jnp.int32, (tq, 1), 0)
    acc = jnp.zeros((tq, D_MODEL), F32)
    for f in range(D_FF // FF_TILE):
        halves = []
        for half in range(2):
            c0 = half * D_FF + f * FF_TILE
            sl = slice(c0, c0 + FF_TILE)
            hh = _dot(h, wup_ref[:, sl])
            cp = carry[:, sl]
            s1 = jnp.where(row == 0, cp[7:8], pltpu.roll(hh, 1, 0))
            s2 = jnp.where(row == 0, cp[6:7], jnp.where(row == 1, cp[7:8], pltpu.roll(hh, 2, 0)))
            carry[:, sl] = hh[tq - 8:, :]
            cw = cw_ref[:, sl]
            halves.append(cb_ref[:, sl] + s2 * cw[0:1] + s1 * cw[1:2] + hh * cw[2:3])
        gate, val = halves
        u = gate * _sigmoid(gate) * val
        acc = acc + _dot(u.astype(BF16), wdn_ref[f * FF_TILE:(f + 1) * FF_TILE, :])
    o_ref[0] = x + acc


def _ffn(x, g, wup, cw, cb, wdn, tq=512):
    b, t, _ = x.shape
    xblk = pl.BlockSpec((1, tq, D_MODEL), lambda i, j: (i, j, 0))

    def full(a):
        return pl.BlockSpec(a.shape, lambda i, j: (0,) * a.ndim, pipeline_mode=pl.Buffered(1))

    return pl.pallas_call(
        functools.partial(_ffn_body, tq),
        grid=(b, t // tq),
        in_specs=[xblk, full(g), full(wup), full(cw), full(cb), full(wdn)],
        out_specs=xblk,
        out_shape=jax.ShapeDtypeStruct(x.shape, F32),
        scratch_shapes=[pltpu.VMEM((8, 2 * D_FF), F32)],
        compiler_params=_params("parallel", "arbitrary"),
        name="conv_ffn",
    )(x, g, wup, cw, cb, wdn)


def _row(v):
    return v.reshape(1, -1)


def kernel(x, mem, norm1_g, w_in, shift_mu, decay_w0, decay_w2, iclr_a0, iclr_a2, gate_g2, k_k, k_a, r_k, lnx_g, lnx_b, vres_v0, vres_v1, vres_v2, sb_q_norm_g, sb_k_norm_g, mem_norm_g, w_mem_kv, mem_q_norm_g, mem_k_norm_g, w_branch, w_out, norm2_g, w_up, conv_w, conv_b, w_down):
    b, t, _ = x.shape
    assert 2 * b * N_HEADS == LANES, "recurrence layout packs (key-half, batch, head) onto the lanes"
    depth = w_in.shape[0]

    head_id = jnp.arange(BRANCH_W) % N_HEADS
    bd = (head_id[:, None] == head_id[None, :]).astype(BF16)
    tri = jnp.arange(LANES)
    u_ext = jnp.concatenate([(tri[:, None] > tri[None, :]).astype(BF16),
                             jnp.ones((LANES, LANES), BF16)], axis=1)
    u_ext = jnp.concatenate([u_ext, u_ext], axis=0)
    zpad64 = jnp.zeros((64, BRANCH_W), F32)

    def rkv_order(a):
        head = _rec_order(a[..., :3 * BRANCH_W].reshape(a.shape[:-1] + (3, BRANCH_W)))
        return jnp.concatenate([head.reshape(a.shape[:-1] + (3 * BRANCH_W,)), a[..., 3 * BRANCH_W:]], axis=-1)

    w_rwkv_all = jnp.pad(rkv_order(w_in[..., :RWKV_COLS]), ((0, 0), (0, 0), (0, RWKV_PAD - RWKV_COLS)))
    w_branch_r = w_branch.at[:, 0].set(_rec_order(w_branch[:, 0], axis=1)).reshape(depth, -1, D_MODEL)

    v_first = None
    for l in range(depth):
        x2d = x.reshape(b * t, D_MODEL)
        P, Pr = _proj(x2d, _row(norm1_g[l]), _to_bf16(w_rwkv_all, l), _to_bf16(w_in, l, RWKV_COLS))
        P, Pr = P.reshape(b, t, RWKV_PAD), Pr.reshape(b, t, REST_COLS)

        vecs = (_row(rkv_order(shift_mu[l])), _row(_rec_order(decay_w0[l])), _row(_rec_order(iclr_a0[l])),
                _row(_rec_order(k_k[l])), _row(_rec_order(k_a[l])), _row(_rec_order(r_k[l].reshape(-1))))
        mats = (jnp.concatenate([_rec_order(decay_w2[l]), zpad64], axis=0).astype(BF16),
                jnp.concatenate([zpad64, _rec_order(iclr_a2[l])], axis=0).astype(BF16),
                _rec_order(gate_g2[l]).astype(BF16))
        vres = None
        if l > 0:
            v1 = jnp.pad(_rec_order(vres_v1[l - 1], axis=0), ((0, 0), (0, LANES - vres_v1.shape[-1]))).astype(BF16)
            v2 = jnp.pad(_rec_order(vres_v2[l - 1]), ((0, LANES - vres_v2.shape[-2]), (0, 0))).astype(BF16)
            vres = (v_first, _row(_rec_order(vres_v0[l - 1])), v1, v2)
        r, dec, k2, v, nkk, bb, g, bonus, *v_plain = _prep(P, vecs, mats, bd, vres)
        if l == 0:
            v_first = v_plain[0]

        kn, mv = _mem_kv(mem, _row(mem_norm_g[l]), _to_bf16(w_mem_kv, l), _row(mem_k_norm_g[l]))
        ym = _mem_attention(Pr, kn, mv, _row(mem_q_norm_g[l]))

        lanes = lambda a: a.reshape(HALF_J, LANES, t)
        y = _recurrence(lanes(dec), lanes(nkk), lanes(bb), lanes(k2), lanes(r), _value_lanes(v), ym)
        y = _from_value_lanes(y, b)

        qg2 = _row(jnp.concatenate([sb_q_norm_g[l], sb_q_norm_g[l]]))
        kg2 = _row(jnp.concatenate([sb_k_norm_g[l], sb_k_norm_g[l]]))
        yb = _sb_attention(Pr, qg2, kg2, u_ext)

        wbr = _to_bf16(w_branch_r, l).reshape(w_branch.shape[1:])
        x = _merge(x, y, bonus, g, yb, ym, Pr, _row(_rec_order(lnx_g[l])), _row(_rec_order(lnx_b[l])), bd, wbr,
                   _to_bf16(w_out, l))
        x = _ffn(x, _row(norm2_g[l]), _to_bf16(w_up, l), conv_w[l], _row(conv_b[l]), _to_bf16(w_down, l))
    return x
```

```python
import functools

import jax
import jax.numpy as jnp
from jax import lax
from jax.experimental import pallas as pl
from jax.experimental.pallas import tpu as pltpu

F32 = jnp.float32
BF16 = jnp.bfloat16

D_MODEL = 1024
N_HEADS = 8
HEAD_DIM = 64
BRANCH_W = 512
RWKV_COLS = 1792
RWKV_PAD = 2048
REST_COLS = 5120
MEM_Q_OFF = 1536
GATE_OFF = 2048
LOG2E = 1.4426950408889634
MEM_LEN = 256
MEM_HEADS = 4
MEM_HD = 128
D_FF = 2816
NORM_EPS = 1e-6
LNX_EPS = 64e-5
LANES = 128
MASKED_LOGIT = -1e30
SB_ROWS = 128
VMEM_LIMIT = 56 * 1024 * 1024

NT_DIMS = (((1,), (1,)), ((), ()))


def _rms(xf, g, eps=NORM_EPS):
    return xf * lax.rsqrt(jnp.mean(xf * xf, axis=-1, keepdims=True) + eps) * g


def _sigmoid(x):
    return 1.0 / (1.0 + jnp.exp(-x))


def _softplus(x):
    return jnp.maximum(x, 0.0) + jnp.log(1.0 + jnp.exp(-jnp.abs(x)))


def _dot(a, b):
    return jnp.dot(a, b, preferred_element_type=F32)


def _split3(x):
    h1 = x.astype(BF16)
    r1 = x - h1.astype(F32)
    h2 = r1.astype(BF16)
    h3 = (r1 - h2.astype(F32)).astype(BF16)
    return h1, h2, h3


def _dot_ones(x, ones_bf16):
    h1, h2, h3 = _split3(x)
    return _dot(h1, ones_bf16) + _dot(h2, ones_bf16) + _dot(h3, ones_bf16)


def _dot_ones2(x, ones_bf16):
    h1, h2, _ = _split3(x)
    return _dot(h1, ones_bf16) + _dot(h2, ones_bf16)


def _params(*sem):
    return pltpu.CompilerParams(dimension_semantics=sem, vmem_limit_bytes=VMEM_LIMIT)


def _cast_body(w_ref, o_ref):
    o_ref[...] = w_ref[...].astype(BF16)


def _to_bf16(w, l, col0=0, ncols=None, out_cols=None):
    _, rows, total = w.shape
    ncols = total - col0 if ncols is None else ncols
    out_cols = ncols if out_cols is None else out_cols
    blk = 2 * LANES
    while all(n % (2 * blk) == 0 for n in (col0, ncols, out_cols)) and blk < 512:
        blk *= 2
    nreal = ncols // blk

    def body(w_ref, o_ref):
        @pl.when(pl.program_id(0) < nreal)
        def _():
            o_ref[...] = w_ref[...].astype(BF16)

        @pl.when(pl.program_id(0) >= nreal)
        def _():
            o_ref[...] = jnp.zeros_like(o_ref)

    return pl.pallas_call(
        body if out_cols > ncols else _cast_body,
        grid=(out_cols // blk,),
        in_specs=[pl.BlockSpec((None, rows, blk), lambda j: (l, 0, col0 // blk + jnp.minimum(j, nreal - 1)))],
        out_specs=pl.BlockSpec((rows, blk), lambda j: (0, j)),
        out_shape=jax.ShapeDtypeStruct((rows, out_cols), BF16),
        compiler_params=_params("parallel"),
        name="to_bf16",
    )(w)


def _proj_body(x_ref, g_ref, wa_ref, wb_ref, oa_ref, ob_ref):
    h = _rms(x_ref[...], g_ref[...]).astype(BF16)
    oa_ref[...] = _dot(h, wa_ref[...])
    ob_ref[...] = _dot(h, wb_ref[...]).astype(BF16)


def _proj(x2d, g, w_rwkv, w_rest, tq=512):
    n = x2d.shape[0]
    ca, cb = w_rwkv.shape[1], w_rest.shape[1]

    def full(a):
        return pl.BlockSpec(a.shape, lambda i: (0, 0), pipeline_mode=pl.Buffered(1))

    return pl.pallas_call(
        _proj_body,
        grid=(n // tq,),
        in_specs=[pl.BlockSpec((tq, D_MODEL), lambda i: (i, 0)), full(g), full(w_rwkv), full(w_rest)],
        out_specs=[pl.BlockSpec((tq, ca), lambda i: (i, 0)), pl.BlockSpec((tq, cb), lambda i: (i, 0))],
        out_shape=[jax.ShapeDtypeStruct((n, ca), F32), jax.ShapeDtypeStruct((n, cb), BF16)],
        compiler_params=_params("parallel"),
        cost_estimate=pl.CostEstimate(flops=2 * n * D_MODEL * (ca + cb), transcendentals=n,
                                      bytes_accessed=4 * n * D_MODEL + 2 * D_MODEL * (ca + cb) + n * (4 * ca + 2 * cb)),
        name="proj",
    )(x2d, g, w_rwkv, w_rest)


def _prep_body(has_vres, tq, *refs):
    if has_vres:
        (p_ref, mu_ref, w0_ref, w2_ref, a0_ref, a2_ref, g2_ref, kk_ref, ka_ref, rk_ref, bd_ref,
         vf_ref, v0_ref, v1_ref, v2_ref,
         r_o, dec_o, k_o, v_o, nkk_o, bb_o, g_o, bonus_o, carry) = refs
        vfirst_o = None
    else:
        (p_ref, mu_ref, w0_ref, w2_ref, a0_ref, a2_ref, g2_ref, kk_ref, ka_ref, rk_ref, bd_ref,
         r_o, dec_o, k_o, v_o, nkk_o, bb_o, g_o, bonus_o, vfirst_o, carry) = refs

    @pl.when(pl.program_id(1) == 0)
    def _():
        carry[...] = jnp.zeros_like(carry)

    p = p_ref[0]
    row = lax.broadcasted_iota(jnp.int32, (tq, 1), 0)
    prev = jnp.where(row == 0, carry[7:8, :], pltpu.roll(p, 1, 0))
    carry[...] = p[tq - 8:, :]
    ps = p + (prev - p) * mu_ref[...]

    r = ps[:, 0:512]
    k = ps[:, 512:1024]
    v = ps[:, 1024:1536]
    wa = ps[:, 1536:1664]
    gl = ps[:, 1664:1792]

    wlog = -_softplus(-(w0_ref[...] + _dot(jnp.tanh(wa).astype(BF16), w2_ref[...]))) - 0.5
    dec = jnp.exp(-jnp.exp(wlog))
    a = _sigmoid(a0_ref[...] + _dot(wa.astype(BF16), a2_ref[...]))
    g = _dot(_sigmoid(gl).astype(BF16), g2_ref[...])
    if has_vres:
        lo = _dot(v.astype(BF16), v1_ref[...])
        v = v + (vf_ref[0] - v) * _sigmoid(v0_ref[...] + _dot(lo.astype(BF16), v2_ref[...]))
    bd = bd_ref[...]
    kk = k * kk_ref[...]
    kk = kk / jnp.maximum(jnp.sqrt(_dot_ones(kk * kk, bd)), 1e-12)
    k2 = k * (1.0 + (a - 1.0) * ka_ref[...])
    bonus = _dot_ones(r * k2 * rk_ref[...], bd) * v

    def rec_layout(x):
        return x.T.reshape(HALF_J, 2, 1, N_HEADS, tq)

    r_o[...] = rec_layout(r)
    dec_o[...] = rec_layout(dec)
    k_o[...] = rec_layout(k2)
    v_o[0] = v.T
    nkk_o[...] = rec_layout(-kk)
    bb_o[...] = rec_layout(kk * a)
    g_o[0] = g.astype(BF16)
    bonus_o[0] = bonus.astype(BF16)
    if vfirst_o is not None:
        vfirst_o[0] = v


def _prep(P, vecs, mats, bd, vres, tq=512):
    b, t, _ = P.shape
    has_vres = vres is not None
    tok = pl.BlockSpec((1, tq, BRANCH_W), lambda i, j: (i, j, 0))

    def full(a):
        return pl.BlockSpec(a.shape, lambda i, j: (0,) * a.ndim)

    mu, w0, a0, kk, ka, rk = vecs
    w2, a2, g2 = mats
    args = [P, mu, w0, w2, a0, a2, g2, kk, ka, rk, bd]
    in_specs = [pl.BlockSpec((1, tq, RWKV_COLS), lambda i, j: (i, j, 0))] + [full(a) for a in args[1:]]
    if has_vres:
        vf, v0, v1, v2 = vres
        args += [vf, v0, v1, v2]
        in_specs += [tok, full(v0), full(v1), full(v2)]
    out = jax.ShapeDtypeStruct((b, t, BRANCH_W), F32)
    out_t = jax.ShapeDtypeStruct((HALF_J, 2, b, N_HEADS, t), F32)
    tok_t = pl.BlockSpec((HALF_J, 2, 1, N_HEADS, tq), lambda i, j: (0, 0, i, 0, j))
    out_v = jax.ShapeDtypeStruct((b, BRANCH_W, t), F32)
    tok_v = pl.BlockSpec((1, BRANCH_W, tq), lambda i, j: (i, 0, j))
    out_h = jax.ShapeDtypeStruct((b, t, BRANCH_W), BF16)
    return pl.pallas_call(
        functools.partial(_prep_body, has_vres, tq),
        grid=(b, t // tq),
        in_specs=in_specs,
        out_specs=[tok_t] * 3 + [tok_v] + [tok_t] * 2 + [tok] * (2 if has_vres else 3),
        out_shape=[out_t] * 3 + [out_v] + [out_t] * 2 + [out_h] * 2 + ([] if has_vres else [out]),
        scratch_shapes=[pltpu.VMEM((8, RWKV_COLS), F32)],
        compiler_params=_params("parallel", "arbitrary"),
        name="rwkv_prep",
    )(*args)


HALF_J = HEAD_DIM // 2
SUBLANES = 8


def _rec_body(tc, w_ref, nk_ref, b_ref, k_ref, r_ref, v_ref, after_ref, y_ref,
              s_scr, w_scr, nk_scr, b_scr, k_scr, r_scr, ypart_scr):
    del after_ref

    @pl.when(pl.program_id(0) == 0)
    def _():
        s_scr[...] = jnp.zeros_like(s_scr)

    for src, dst in ((w_ref, w_scr), (nk_ref, nk_scr), (b_ref, b_scr), (k_ref, k_scr), (r_ref, r_scr)):
        for j in range(HALF_J):
            dst[j] = src[j].T

    groups = [pl.ds(g * SUBLANES, SUBLANES) for g in range(HEAD_DIM // SUBLANES)]

    def fold(x):
        return x + pltpu.roll(x, HEAD_DIM, 1)

    def step(t, carry):
        part = [None] * len(groups)
        for j in range(HALF_J):
            nk = nk_scr[j, pl.ds(t, 1), :]
            for g, rows in enumerate(groups):
                term = s_scr[j, rows, :] * nk
                part[g] = term if part[g] is None else part[g] + term
        sa = [fold(p) for p in part]
        v = [jnp.concatenate([v_ref[t, rows, :]] * 2, axis=-1) for rows in groups]
        yacc = [None] * len(groups)
        for j in range(HALF_J):
            w = w_scr[j, pl.ds(t, 1), :]
            bb = b_scr[j, pl.ds(t, 1), :]
            kk = k_scr[j, pl.ds(t, 1), :]
            rr = r_scr[j, pl.ds(t, 1), :]
            for g, rows in enumerate(groups):
                sn = s_scr[j, rows, :] * w + (sa[g] * bb + v[g] * kk)
                s_scr[j, rows, :] = sn
                term = sn * rr
                yacc[g] = term if yacc[g] is None else yacc[g] + term
        for g, rows in enumerate(groups):
            ypart_scr[t, rows, :] = yacc[g]
        return carry

    lax.fori_loop(0, tc, step, 0)
    y = fold(ypart_scr[...].reshape(tc * HEAD_DIM, LANES))
    y_ref[...] = y[:, :HEAD_DIM].reshape(tc, HEAD_DIM, HEAD_DIM)


def _recurrence(dec, nkk, bb, k2, r, v, after, tc=128):
    t = dec.shape[-1]
    kspec = pl.BlockSpec((HALF_J, LANES, tc), lambda i: (0, 0, i))
    vspec = pl.BlockSpec((tc, HEAD_DIM, HEAD_DIM), lambda i: (i, 0, 0))
    turned = pltpu.VMEM((HALF_J, tc, LANES), F32)
    return pl.pallas_call(
        functools.partial(_rec_body, tc),
        grid=(t // tc,),
        in_specs=[kspec] * 5 + [vspec, pl.BlockSpec(memory_space=pl.ANY)],
        out_specs=vspec,
        out_shape=jax.ShapeDtypeStruct((t, HEAD_DIM, HEAD_DIM), F32),
        scratch_shapes=[pltpu.VMEM((HALF_J, HEAD_DIM, LANES), F32)] + [turned] * 5
                       + [pltpu.VMEM((tc, HEAD_DIM, LANES), F32)],
        compiler_params=_params("arbitrary"),
        cost_estimate=pl.CostEstimate(flops=9 * t * HALF_J * HEAD_DIM * LANES, transcendentals=0,
                                      bytes_accessed=4 * t * (5 * HALF_J * LANES + 2 * HEAD_DIM * HEAD_DIM)),
        name="rwkv_rec",
    )(dec, nkk, bb, k2, r, v, after)


def _value_lanes(v_t):
    b, _, t = v_t.shape
    return v_t.reshape(b, HALF_J, 2, N_HEADS, t).transpose(4, 2, 1, 0, 3).reshape(t, HEAD_DIM, b * N_HEADS)


def _from_value_lanes(y, b):
    t = y.shape[0]
    return y.reshape(t, 2, HALF_J, b, N_HEADS).transpose(3, 2, 1, 4, 0).reshape(b, BRANCH_W, t)


def _rec_order(a, axis=-1):
    a = jnp.moveaxis(a, axis, -1)
    lead = a.shape[:-1]
    a = jnp.swapaxes(a.reshape(lead + (N_HEADS, 2, HALF_J)), -1, -3).reshape(lead + (BRANCH_W,))
    return jnp.moveaxis(a, -1, axis)


def _sb_body(tq, tk, q_ref, k_ref, v_ref, qg_ref, kg_ref, u_ref, o_ref,
             kn_scr, vb_scr, qn_scr, hl_scr, lsig_scr, logit_scr, run_scr, acc_scr):
    qi = pl.program_id(2)
    nsub = tq // tk
    nblk = kn_scr.shape[0]
    lane = lax.broadcasted_iota(jnp.int32, (1, LANES), 1)
    first = lane < HEAD_DIM
    m_a = first.astype(F32)
    m_b = 1.0 - m_a

    def headnorm(x, g):
        x2 = x * x
        s_a = jnp.sum(x2 * m_a, axis=-1, keepdims=True)
        s_b = jnp.sum(x2 * m_b, axis=-1, keepdims=True)
        ms = jnp.where(first, s_a, s_b) * (1.0 / HEAD_DIM)
        return x * lax.rsqrt(ms + NORM_EPS) * g

    @pl.when(qi == 0)
    def _():
        kn = headnorm(k_ref[0].astype(F32), kg_ref[...])
        v = v_ref[0].astype(F32)
        kn_scr[:, :tk, :] = (kn * m_a).astype(BF16).reshape(nblk, tk, LANES)
        kn_scr[:, tk:, :] = (kn * m_b).astype(BF16).reshape(nblk, tk, LANES)
        vb_scr[:, :tk, :] = (v * m_a).astype(BF16).reshape(nblk, tk, LANES)
        vb_scr[:, tk:, :] = (v * m_b).astype(BF16).reshape(nblk, tk, LANES)

    qn_scr[...] = (headnorm(q_ref[0].astype(F32), qg_ref[...]) * (LOG2E * HEAD_DIM ** -0.5)).astype(BF16)
    run_scr[...] = jnp.zeros_like(run_scr)
    acc_scr[...] = jnp.zeros_like(acc_scr)
    uu = u_ref[...]
    rowi = lax.broadcasted_iota(jnp.int32, (SB_ROWS, 2 * tk), 0)
    coli = lax.broadcasted_iota(jnp.int32, (SB_ROWS, 2 * tk), 1) & (tk - 1)

    last = qi * nsub + (nsub - 1)
    heads = (slice(0, tk), slice(tk, 2 * tk))
    groups = [slice(s * SB_ROWS, (s + 1) * SB_ROWS) for s in range(tq // SB_ROWS)]

    def score(m, masked, rows):
        z = lax.dot_general(qn_scr[rows, :], kn_scr[jnp.maximum(last - m, 0)], NT_DIMS,
                            preferred_element_type=F32)
        l1m = -(jnp.maximum(z, 0.0) + jnp.log2(1.0 + jnp.exp2(-jnp.abs(z))))
        lsig = z + l1m
        if masked:
            mask = (coli + ((nsub - 1 - m) * tk - rows.start)) < rowi
            l1m = jnp.where(mask, l1m, 0.0)
            lsig = jnp.where(mask, lsig, MASKED_LOGIT)
        hi = l1m.astype(BF16)
        lo = (l1m - hi.astype(F32)).astype(BF16)
        for h, sl in enumerate(heads):
            hl_scr[h, rows, :] = jnp.concatenate([hi[:, sl], lo[:, sl]], axis=1)
        lsig_scr[rows, :] = lsig

    def suffix(rows):
        for h, sl in enumerate(heads):
            ce = _dot(hl_scr[h, rows, :], uu)
            run = run_scr[h, rows, :]
            logit_scr[rows, sl] = lsig_scr[rows, sl] + ce[:, :tk] + run
            run_scr[h, rows, :] = run + ce[:, tk:]

    def weigh(m, rows):
        acc_scr[rows, :] += _dot(jnp.exp2(logit_scr[rows, :]).astype(BF16), vb_scr[last - m])

    def body(m, carry):
        for rows in groups:
            weigh(m, rows)
            suffix(rows)
            score(m + 2, False, rows)
        return carry

    def live(rows, m):
        return (nsub - 1 - m) * tk <= rows.stop - 2

    def cut(rows, m):
        return (nsub - m) * tk - 1 >= rows.start

    for rows in groups:
        if live(rows, 0):
            score(0, cut(rows, 0), rows)
    for rows in groups:
        if live(rows, 0):
            suffix(rows)
        if live(rows, 1):
            score(1, cut(rows, 1), rows)
    for m in range(nsub):
        for rows in groups:
            if live(rows, m):
                weigh(m, rows)
            if live(rows, m + 1):
                suffix(rows)
            if live(rows, m + 2):
                score(m + 2, cut(rows, m + 2), rows)
    lax.fori_loop(nsub, (qi + 1) * nsub, body, 0)
    o_ref[0] = acc_scr[...]


def _sb_attention(P, qg, kg, uu, tq=1024, tk=LANES):
    b, t, _ = P.shape
    tq = min(tq, t)
    npair = N_HEADS // 2
    qoff, koff, voff = 0, BRANCH_W // LANES, 2 * BRANCH_W // LANES
    stacked = pltpu.VMEM((t // tk, 2 * tk, LANES), BF16)
    return pl.pallas_call(
        functools.partial(_sb_body, tq, tk),
        grid=(b, npair, t // tq),
        in_specs=[
            pl.BlockSpec((1, tq, LANES), lambda i, h, j: (i, j, qoff + h)),
            pl.BlockSpec((1, t, LANES), lambda i, h, j: (i, 0, koff + h)),
            pl.BlockSpec((1, t, LANES), lambda i, h, j: (i, 0, voff + h)),
            pl.BlockSpec((1, LANES), lambda i, h, j: (0, 0)),
            pl.BlockSpec((1, LANES), lambda i, h, j: (0, 0)),
            pl.BlockSpec((2 * tk, 2 * tk), lambda i, h, j: (0, 0)),
        ],
        out_specs=pl.BlockSpec((1, tq, LANES), lambda i, h, j: (i, j, h)),
        out_shape=jax.ShapeDtypeStruct((b, t, BRANCH_W), F32),
        scratch_shapes=[stacked, stacked,
                        pltpu.VMEM((tq, LANES), BF16),
                        pltpu.VMEM((2, tq, 2 * tk), BF16),
                        pltpu.VMEM((tq, 2 * tk), F32),
                        pltpu.VMEM((tq, 2 * tk), F32),
                        pltpu.VMEM((2, tq, LANES), F32),
                        pltpu.VMEM((tq, LANES), F32)],
        compiler_params=_params("parallel", "parallel", "arbitrary"),
        cost_estimate=pl.CostEstimate(flops=b * N_HEADS * t * t * (2 * HEAD_DIM + 4 * tk + 2 * HEAD_DIM) // 2,
                                      transcendentals=3 * b * N_HEADS * t * t // 2,
                                      bytes_accessed=2 * 3 * b * t * BRANCH_W + 4 * b * t * BRANCH_W),
        name="sb_attn",
    )(P, P, P, qg, kg, uu)


def _memkv_body(m_ref, g_ref, w_ref, kg_ref, k_o, v_o):
    kv = _dot(_rms(m_ref[0], g_ref[...]).astype(BF16), w_ref[...])
    for h in range(MEM_HEADS):
        sl = slice(h * MEM_HD, (h + 1) * MEM_HD)
        k_o[0, :, sl] = _rms(kv[:, sl], kg_ref[...]).astype(BF16)
    v_o[0] = kv[:, BRANCH_W:].astype(BF16)


def _mem_kv(mem, g, w_bf16, kg):
    b = mem.shape[0]
    out = jax.ShapeDtypeStruct((b, MEM_LEN, BRANCH_W), BF16)
    blk = pl.BlockSpec((1, MEM_LEN, BRANCH_W), lambda i: (i, 0, 0))
    return pl.pallas_call(
        _memkv_body,
        grid=(b,),
        in_specs=[
            pl.BlockSpec((1, MEM_LEN, D_MODEL), lambda i: (i, 0, 0)),
            pl.BlockSpec((1, D_MODEL), lambda i: (0, 0)),
            pl.BlockSpec((D_MODEL, 2 * BRANCH_W), lambda i: (0, 0)),
            pl.BlockSpec((1, MEM_HD), lambda i: (0, 0)),
        ],
        out_specs=[blk, blk],
        out_shape=[out, out],
        compiler_params=_params("parallel"),
        name="mem_kv",
    )(mem, g, w_bf16, kg)


def _memattn_body(q_ref, k_ref, v_ref, qg_ref, o_ref):
    q = q_ref[0].astype(F32)
    for h in range(MEM_HEADS):
        sl = slice(h * MEM_HD, (h + 1) * MEM_HD)
        qh = (_rms(q[:, sl], qg_ref[...]) * (MEM_HD ** -0.5)).astype(BF16)
        s = lax.dot_general(qh, k_ref[0, :, sl], NT_DIMS, preferred_element_type=F32)
        p = jnp.exp(s - jnp.max(s, axis=-1, keepdims=True))
        den = jnp.sum(p, axis=-1, keepdims=True)
        o_ref[0, :, sl] = _dot(p.astype(BF16), v_ref[0, :, sl]) / den


def _mem_attention(P, kn, mv, qg, tq=1024):
    b, t, _ = P.shape
    qoff = MEM_Q_OFF // BRANCH_W
    kv = pl.BlockSpec((1, MEM_LEN, BRANCH_W), lambda i, j: (i, 0, 0))
    return pl.pallas_call(
        _memattn_body,
        grid=(b, t // tq),
        in_specs=[
            pl.BlockSpec((1, tq, BRANCH_W), lambda i, j: (i, j, qoff)),
            kv, kv,
            pl.BlockSpec((1, MEM_HD), lambda i, j: (0, 0)),
        ],
        out_specs=pl.BlockSpec((1, tq, BRANCH_W), lambda i, j: (i, j, 0)),
        out_shape=jax.ShapeDtypeStruct((b, t, BRANCH_W), F32),
        compiler_params=_params("parallel", "parallel"),
        name="mem_attn",
    )(P, kn, mv, qg)


def _merge_body(x_ref, y_ref, bonus_ref, g_ref, yb_ref, ym_ref, g0_ref, g1_ref, g2_ref,
                lng_ref, lnb_ref, bd_ref, wbr_ref, wout_ref, o_ref):
    bd = bd_ref[...]
    y = y_ref[0].T
    d = y - _dot_ones2(y, bd) * (1.0 / HEAD_DIM)
    var = _dot_ones2(d * d, bd) * (1.0 / HEAD_DIM)
    ya = (d * lax.rsqrt(var + LNX_EPS) * lng_ref[...] + lnb_ref[...] + bonus_ref[0].astype(F32)) * g_ref[0].astype(F32)
    m = (_sigmoid(g0_ref[0].astype(F32)) * _dot(ya.astype(BF16), wbr_ref[0])
         + _sigmoid(g1_ref[0].astype(F32)) * _dot(yb_ref[0].astype(BF16), wbr_ref[1])
         + _sigmoid(g2_ref[0].astype(F32)) * _dot(ym_ref[0].astype(BF16), wbr_ref[2]))
    o_ref[0] = x_ref[0] + _dot(m.astype(BF16), wout_ref[...])


def _merge(x, y, bonus, g, yb, ym, gates, lng, lnb, bd, wbr, wout, tq=1024):
    b, t, _ = x.shape
    tok = pl.BlockSpec((1, tq, BRANCH_W), lambda i, j: (i, j, 0))
    xblk = pl.BlockSpec((1, tq, D_MODEL), lambda i, j: (i, j, 0))
    goff = GATE_OFF // D_MODEL

    def gate(n):
        return pl.BlockSpec((1, tq, D_MODEL), lambda i, j: (i, j, goff + n))

    def full(a):
        return pl.BlockSpec(a.shape, lambda i, j: (0,) * a.ndim)

    return pl.pallas_call(
        _merge_body,
        grid=(b, t // tq),
        in_specs=[xblk, pl.BlockSpec((1, BRANCH_W, tq), lambda i, j: (i, 0, j)), tok, tok, tok, tok,
                  gate(0), gate(1), gate(2),
                  full(lng), full(lnb), full(bd), full(wbr), full(wout)],
        out_specs=xblk,
        out_shape=jax.ShapeDtypeStruct(x.shape, F32),
        compiler_params=_params("parallel", "parallel"),
        name="merge",
    )(x, y, bonus, g, yb, ym, gates, gates, gates, lng, lnb, bd, wbr, wout)


FF_TILE = 1408


def _ffn_body(tq, x_ref, g_ref, wup_ref, cw_ref, cb_ref, wdn_ref, o_ref, carry):
    @pl.when(pl.program_id(1) == 0)
    def _():
        carry[...] = jnp.zeros_like(carry)

    x = x_ref[0]
    h = _rms(x, g_ref[...]).astype(BF16)
    row = lax.broadcasted_iota(jnp.int32, (tq, 1), 0)
    acc = jnp.zeros((tq, D_MODEL), F32)
    for f in range(D_FF // FF_TILE):
        halves = []
        for half in range(2):
            c0 = half * D_FF + f * FF_TILE
            sl = slice(c0, c0 + FF_TILE)
            hh = _dot(h, wup_ref[:, sl])
            cp = carry[:, sl]
            s1 = jnp.where(row == 0, cp[7:8], pltpu.roll(hh, 1, 0))
            s2 = jnp.where(row == 0, cp[6:7], jnp.where(row == 1, cp[7:8], pltpu.roll(hh, 2, 0)))
            carry[:, sl] = hh[tq - 8:, :]
            cw = cw_ref[:, sl]
            halves.append(cb_ref[:, sl] + s2 * cw[0:1] + s1 * cw[1:2] + hh * cw[2:3])
        gate, val = halves
        u = gate * _sigmoid(gate) * val
        acc = acc + _dot(u.astype(BF16), wdn_ref[f * FF_TILE:(f + 1) * FF_TILE, :])
    o_ref[0] = x + acc


def _ffn(x, g, wup, cw, cb, wdn, tq=512):
    b, t, _ = x.shape
    xblk = pl.BlockSpec((1, tq, D_MODEL), lambda i, j: (i, j, 0))

    def full(a):
        return pl.BlockSpec(a.shape, lambda i, j: (0,) * a.ndim, pipeline_mode=pl.Buffered(1))

    return pl.pallas_call(
        functools.partial(_ffn_body, tq),
        grid=(b, t // tq),
        in_specs=[xblk, full(g), full(wup), full(cw), full(cb), full(wdn)],
        out_specs=xblk,
        out_shape=jax.ShapeDtypeStruct(x.shape, F32),
        scratch_shapes=[pltpu.VMEM((8, 2 * D_FF), F32)],
        compiler_params=_params("parallel", "arbitrary"),
        name="conv_ffn",
    )(x, g, wup, cw, cb, wdn)


def _row(v):
    return v.reshape(1, -1)


def kernel(x, mem, norm1_g, w_in, shift_mu, decay_w0, decay_w2, iclr_a0, iclr_a2, gate_g2, k_k, k_a, r_k, lnx_g, lnx_b, vres_v0, vres_v1, vres_v2, sb_q_norm_g, sb_k_norm_g, mem_norm_g, w_mem_kv, mem_q_norm_g, mem_k_norm_g, w_branch, w_out, norm2_g, w_up, conv_w, conv_b, w_down):
    b, t, _ = x.shape
    assert 2 * b * N_HEADS == LANES, "recurrence layout packs (key-half, batch, head) onto the lanes"
    depth = w_in.shape[0]

    head_id = jnp.arange(BRANCH_W) % N_HEADS
    bd = (head_id[:, None] == head_id[None, :]).astype(BF16)
    tri = jnp.arange(LANES)
    u_ext = jnp.concatenate([(tri[:, None] > tri[None, :]).astype(BF16),
                             jnp.ones((LANES, LANES), BF16)], axis=1)
    u_ext = jnp.concatenate([u_ext, u_ext], axis=0)
    zpad64 = jnp.zeros((64, BRANCH_W), F32)

    def rkv_order(a):
        head = _rec_order(a[..., :3 * BRANCH_W].reshape(a.shape[:-1] + (3, BRANCH_W)))
        return jnp.concatenate([head.reshape(a.shape[:-1] + (3 * BRANCH_W,)), a[..., 3 * BRANCH_W:]], axis=-1)

    w_rwkv_all = jnp.pad(rkv_order(w_in[..., :RWKV_COLS]), ((0, 0), (0, 0), (0, RWKV_PAD - RWKV_COLS)))
    w_branch_r = w_branch.at[:, 0].set(_rec_order(w_branch[:, 0], axis=1)).reshape(depth, -1, D_MODEL)

    v_first = None
    for l in range(depth):
        x2d = x.reshape(b * t, D_MODEL)
        P, Pr = _proj(x2d, _row(norm1_g[l]), _to_bf16(w_rwkv_all, l), _to_bf16(w_in, l, RWKV_COLS))
        P, Pr = P.reshape(b, t, RWKV_PAD), Pr.reshape(b, t, REST_COLS)

        vecs = (_row(rkv_order(shift_mu[l])), _row(_rec_order(decay_w0[l])), _row(_rec_order(iclr_a0[l])),
                _row(_rec_order(k_k[l])), _row(_rec_order(k_a[l])), _row(_rec_order(r_k[l].reshape(-1))))
        mats = (jnp.concatenate([_rec_order(decay_w2[l]), zpad64], axis=0).astype(BF16),
                jnp.concatenate([zpad64, _rec_order(iclr_a2[l])], axis=0).astype(BF16),
                _rec_order(gate_g2[l]).astype(BF16))
        vres = None
        if l > 0:
            v1 = jnp.pad(_rec_order(vres_v1[l - 1], axis=0), ((0, 0), (0, LANES - vres_v1.shape[-1]))).astype(BF16)
            v2 = jnp.pad(_rec_order(vres_v2[l - 1]), ((0, LANES - vres_v2.shape[-2]), (0, 0))).astype(BF16)
            vres = (v_first, _row(_rec_order(vres_v0[l - 1])), v1, v2)
        r, dec, k2, v, nkk, bb, g, bonus, *v_plain = _prep(P, vecs, mats, bd, vres)
        if l == 0:
            v_first = v_plain[0]

        kn, mv = _mem_kv(mem, _row(mem_norm_g[l]), _to_bf16(w_mem_kv, l), _row(mem_k_norm_g[l]))
        ym = _mem_attention(Pr, kn, mv, _row(mem_q_norm_g[l]))

        lanes = lambda a: a.reshape(HALF_J, LANES, t)
        y = _recurrence(lanes(dec), lanes(nkk), lanes(bb), lanes(k2), lanes(r), _value_lanes(v), ym)
        y = _from_value_lanes(y, b)

        qg2 = _row(jnp.concatenate([sb_q_norm_g[l], sb_q_norm_g[l]]))
        kg2 = _row(jnp.concatenate([sb_k_norm_g[l], sb_k_norm_g[l]]))
        yb = _sb_attention(Pr, qg2, kg2, u_ext)

        wbr = _to_bf16(w_branch_r, l).reshape(w_branch.shape[1:])
        x = _merge(x, y, bonus, g, yb, ym, Pr, _row(_rec_order(lnx_g[l])), _row(_rec_order(lnx_b[l])), bd, wbr,
                   _to_bf16(w_out, l))
        x = _ffn(x, _row(norm2_g[l]), _to_bf16(w_up, l), conv_w[l], _row(conv_b[l]), _to_bf16(w_down, l))
    return x
```

```python
import functools

import jax
import jax.numpy as jnp
from jax import lax
from jax.experimental import pallas as pl
from jax.experimental.pallas import tpu as pltpu

F32 = jnp.float32
BF16 = jnp.bfloat16

D_MODEL = 1024
N_HEADS = 8
HEAD_DIM = 64
BRANCH_W = 512
RWKV_COLS = 1792
RWKV_PAD = 2048
REST_COLS = 5120
MEM_Q_OFF = 1536
GATE_OFF = 2048
LOG2E = 1.4426950408889634
MEM_LEN = 256
MEM_HEADS = 4
MEM_HD = 128
D_FF = 2816
NORM_EPS = 1e-6
LNX_EPS = 64e-5
LANES = 128
MASKED_LOGIT = -1e30
SB_ROWS = 128
VMEM_LIMIT = 56 * 1024 * 1024

NT_DIMS = (((1,), (1,)), ((), ()))


def _rms(xf, g, eps=NORM_EPS):
    return xf * lax.rsqrt(jnp.mean(xf * xf, axis=-1, keepdims=True) + eps) * g


def _sigmoid(x):
    return 1.0 / (1.0 + jnp.exp(-x))


def _softplus(x):
    return jnp.maximum(x, 0.0) + jnp.log(1.0 + jnp.exp(-jnp.abs(x)))


def _dot(a, b):
    return jnp.dot(a, b, preferred_element_type=F32)


def _split3(x):
    h1 = x.astype(BF16)
    r1 = x - h1.astype(F32)
    h2 = r1.astype(BF16)
    h3 = (r1 - h2.astype(F32)).astype(BF16)
    return h1, h2, h3


def _dot_ones(x, ones_bf16):
    h1, h2, h3 = _split3(x)
    return _dot(h1, ones_bf16) + _dot(h2, ones_bf16) + _dot(h3, ones_bf16)


def _dot_ones2(x, ones_bf16):
    h1, h2, _ = _split3(x)
    return _dot(h1, ones_bf16) + _dot(h2, ones_bf16)


def _params(*sem):
    return pltpu.CompilerParams(dimension_semantics=sem, vmem_limit_bytes=VMEM_LIMIT)


def _cast_body(w_ref, o_ref):
    o_ref[...] = w_ref[...].astype(BF16)


def _to_bf16(w, l, col0=0, ncols=None, out_cols=None):
    _, rows, total = w.shape
    ncols = total - col0 if ncols is None else ncols
    out_cols = ncols if out_cols is None else out_cols
    blk = 2 * LANES
    while all(n % (2 * blk) == 0 for n in (col0, ncols, out_cols)) and blk < 512:
        blk *= 2
    nreal = ncols // blk

    def body(w_ref, o_ref):
        @pl.when(pl.program_id(0) < nreal)
        def _():
            o_ref[...] = w_ref[...].astype(BF16)

        @pl.when(pl.program_id(0) >= nreal)
        def _():
            o_ref[...] = jnp.zeros_like(o_ref)

    return pl.pallas_call(
        body if out_cols > ncols else _cast_body,
        grid=(out_cols // blk,),
        in_specs=[pl.BlockSpec((None, rows, blk), lambda j: (l, 0, col0 // blk + jnp.minimum(j, nreal - 1)))],
        out_specs=pl.BlockSpec((rows, blk), lambda j: (0, j)),
        out_shape=jax.ShapeDtypeStruct((rows, out_cols), BF16),
        compiler_params=_params("parallel"),
        name="to_bf16",
    )(w)


def _proj_body(x_ref, g_ref, wa_ref, wb_ref, oa_ref, ob_ref):
    h = _rms(x_ref[...], g_ref[...]).astype(BF16)
    oa_ref[...] = _dot(h, wa_ref[...])
    ob_ref[...] = _dot(h, wb_ref[...]).astype(BF16)


def _proj(x2d, g, w_rwkv, w_rest, tq=512):
    n = x2d.shape[0]
    ca, cb = w_rwkv.shape[1], w_rest.shape[1]

    def full(a):
        return pl.BlockSpec(a.shape, lambda i: (0, 0), pipeline_mode=pl.Buffered(1))

    return pl.pallas_call(
        _proj_body,
        grid=(n // tq,),
        in_specs=[pl.BlockSpec((tq, D_MODEL), lambda i: (i, 0)), full(g), full(w_rwkv), full(w_rest)],
        out_specs=[pl.BlockSpec((tq, ca), lambda i: (i, 0)), pl.BlockSpec((tq, cb), lambda i: (i, 0))],
        out_shape=[jax.ShapeDtypeStruct((n, ca), F32), jax.ShapeDtypeStruct((n, cb), BF16)],
        compiler_params=_params("parallel"),
        cost_estimate=pl.CostEstimate(flops=2 * n * D_MODEL * (ca + cb), transcendentals=n,
                                      bytes_accessed=4 * n * D_MODEL + 2 * D_MODEL * (ca + cb) + n * (4 * ca + 2 * cb)),
        name="proj",
    )(x2d, g, w_rwkv, w_rest)


def _prep_body(has_vres, tq, *refs):
    if has_vres:
        (p_ref, mu_ref, w0_ref, w2_ref, a0_ref, a2_ref, g2_ref, kk_ref, ka_ref, rk_ref, bd_ref,
         vf_ref, v0_ref, v1_ref, v2_ref,
         r_o, dec_o, k_o, v_o, nkk_o, bb_o, g_o, bonus_o, carry) = refs
        vfirst_o = None
    else:
        (p_ref, mu_ref, w0_ref, w2_ref, a0_ref, a2_ref, g2_ref, kk_ref, ka_ref, rk_ref, bd_ref,
         r_o, dec_o, k_o, v_o, nkk_o, bb_o, g_o, bonus_o, vfirst_o, carry) = refs

    @pl.when(pl.program_id(1) == 0)
    def _():
        carry[...] = jnp.zeros_like(carry)

    p = p_ref[0]
    row = lax.broadcasted_iota(jnp.int32, (tq, 1), 0)
    prev = jnp.where(row == 0, carry[7:8, :], pltpu.roll(p, 1, 0))
    carry[...] = p[tq - 8:, :]
    ps = p + (prev - p) * mu_ref[...]

    r = ps[:, 0:512]
    k = ps[:, 512:1024]
    v = ps[:, 1024:1536]
    wa = ps[:, 1536:1664]
    gl = ps[:, 1664:1792]

    wlog = -_softplus(-(w0_ref[...] + _dot(jnp.tanh(wa).astype(BF16), w2_ref[...]))) - 0.5
    dec = jnp.exp(-jnp.exp(wlog))
    a = _sigmoid(a0_ref[...] + _dot(wa.astype(BF16), a2_ref[...]))
    g = _dot(_sigmoid(gl).astype(BF16), g2_ref[...])
    if has_vres:
        lo = _dot(v.astype(BF16), v1_ref[...])
        v = v + (vf_ref[0] - v) * _sigmoid(v0_ref[...] + _dot(lo.astype(BF16), v2_ref[...]))
    bd = bd_ref[...]
    kk = k * kk_ref[...]
    kk = kk / jnp.maximum(jnp.sqrt(_dot_ones(kk * kk, bd)), 1e-12)
    k2 = k * (1.0 + (a - 1.0) * ka_ref[...])
    bonus = _dot_ones(r * k2 * rk_ref[...], bd) * v

    def rec_layout(x):
        return x.T.reshape(HALF_J, 2, 1, N_HEADS, tq)

    r_o[...] = rec_layout(r)
    dec_o[...] = rec_layout(dec)
    k_o[...] = rec_layout(k2)
    v_o[0] = v.T
    nkk_o[...] = rec_layout(-kk)
    bb_o[...] = rec_layout(kk * a)
    g_o[0] = g.astype(BF16)
    bonus_o[0] = bonus.astype(BF16)
    if vfirst_o is not None:
        vfirst_o[0] = v


def _prep(P, vecs, mats, bd, vres, tq=512):
    b, t, _ = P.shape
    has_vres = vres is not None
    tok = pl.BlockSpec((1, tq, BRANCH_W), lambda i, j: (i, j, 0))

    def full(a):
        return pl.BlockSpec(a.shape, lambda i, j: (0,) * a.ndim)

    mu, w0, a0, kk, ka, rk = vecs
    w2, a2, g2 = mats
    args = [P, mu, w0, w2, a0, a2, g2, kk, ka, rk, bd]
    in_specs = [pl.BlockSpec((1, tq, RWKV_COLS), lambda i, j: (i, j, 0))] + [full(a) for a in args[1:]]
    if has_vres:
        vf, v0, v1, v2 = vres
        args += [vf, v0, v1, v2]
        in_specs += [tok, full(v0), full(v1), full(v2)]
    out = jax.ShapeDtypeStruct((b, t, BRANCH_W), F32)
    out_t = jax.ShapeDtypeStruct((HALF_J, 2, b, N_HEADS, t), F32)
    tok_t = pl.BlockSpec((HALF_J, 2, 1, N_HEADS, tq), lambda i, j: (0, 0, i, 0, j))
    out_v = jax.ShapeDtypeStruct((b, BRANCH_W, t), F32)
    tok_v = pl.BlockSpec((1, BRANCH_W, tq), lambda i, j: (i, 0, j))
    out_h = jax.ShapeDtypeStruct((b, t, BRANCH_W), BF16)
    return pl.pallas_call(
        functools.partial(_prep_body, has_vres, tq),
        grid=(b, t // tq),
        in_specs=in_specs,
        out_specs=[tok_t] * 3 + [tok_v] + [tok_t] * 2 + [tok] * (2 if has_vres else 3),
        out_shape=[out_t] * 3 + [out_v] + [out_t] * 2 + [out_h] * 2 + ([] if has_vres else [out]),
        scratch_shapes=[pltpu.VMEM((8, RWKV_COLS), F32)],
        compiler_params=_params("parallel", "arbitrary"),
        name="rwkv_prep",
    )(*args)


HALF_J = HEAD_DIM // 2
SUBLANES = 8


def _rec_body(tc, w_ref, nk_ref, b_ref, k_ref, r_ref, v_ref, after_ref, y_ref,
              s_scr, w_scr, nk_scr, b_scr, k_scr, r_scr, ypart_scr):
    del after_ref

    @pl.when(pl.program_id(0) == 0)
    def _():
        s_scr[...] = jnp.zeros_like(s_scr)

    for src, dst in ((w_ref, w_scr), (nk_ref, nk_scr), (b_ref, b_scr), (k_ref, k_scr), (r_ref, r_scr)):
        for j in range(HALF_J):
            dst[j] = src[j].T

    groups = [pl.ds(g * SUBLANES, SUBLANES) for g in range(HEAD_DIM // SUBLANES)]

    def fold(x):
        return x + pltpu.roll(x, HEAD_DIM, 1)

    def step(t, carry):
        part = [None] * len(groups)
        for j in range(HALF_J):
            nk = nk_scr[j, pl.ds(t, 1), :]
            for g, rows in enumerate(groups):
                term = s_scr[j, rows, :] * nk
                part[g] = term if part[g] is None else part[g] + term
        sa = [fold(p) for p in part]
        v = [jnp.concatenate([v_ref[t, rows, :]] * 2, axis=-1) for rows in groups]
        yacc = [None] * len(groups)
        for j in range(HALF_J):
            w = w_scr[j, pl.ds(t, 1), :]
            bb = b_scr[j, pl.ds(t, 1), :]
            kk = k_scr[j, pl.ds(t, 1), :]
            rr = r_scr[j, pl.ds(t, 1), :]
            for g, rows in enumerate(groups):
                sn = s_scr[j, rows, :] * w + (sa[g] * bb + v[g] * kk)
                s_scr[j, rows, :] = sn
                term = sn * rr
                yacc[g] = term if yacc[g] is None else yacc[g] + term
        for g, rows in enumerate(groups):
            ypart_scr[t, rows, :] = yacc[g]
        return carry

    lax.fori_loop(0, tc, step, 0)
    y = fold(ypart_scr[...].reshape(tc * HEAD_DIM, LANES))
    y_ref[...] = y[:, :HEAD_DIM].reshape(tc, HEAD_DIM, HEAD_DIM)


def _recurrence(dec, nkk, bb, k2, r, v, after, tc=128):
    t = dec.shape[-1]
    kspec = pl.BlockSpec((HALF_J, LANES, tc), lambda i: (0, 0, i))
    vspec = pl.BlockSpec((tc, HEAD_DIM, HEAD_DIM), lambda i: (i, 0, 0))
    turned = pltpu.VMEM((HALF_J, tc, LANES), F32)
    return pl.pallas_call(
        functools.partial(_rec_body, tc),
        grid=(t // tc,),
        in_specs=[kspec] * 5 + [vspec, pl.BlockSpec(memory_space=pl.ANY)],
        out_specs=vspec,
        out_shape=jax.ShapeDtypeStruct((t, HEAD_DIM, HEAD_DIM), F32),
        scratch_shapes=[pltpu.VMEM((HALF_J, HEAD_DIM, LANES), F32)] + [turned] * 5
                       + [pltpu.VMEM((tc, HEAD_DIM, LANES), F32)],
        compiler_params=_params("arbitrary"),
        cost_estimate=pl.CostEstimate(flops=9 * t * HALF_J * HEAD_DIM * LANES, transcendentals=0,
                                      bytes_accessed=4 * t * (5 * HALF_J * LANES + 2 * HEAD_DIM * HEAD_DIM)),
        name="rwkv_rec",
    )(dec, nkk, bb, k2, r, v, after)


def _value_lanes(v_t):
    b, _, t = v_t.shape
    return v_t.reshape(b, HALF_J, 2, N_HEADS, t).transpose(4, 2, 1, 0, 3).reshape(t, HEAD_DIM, b * N_HEADS)


def _from_value_lanes(y, b):
    t = y.shape[0]
    return y.reshape(t, 2, HALF_J, b, N_HEADS).transpose(3, 2, 1, 4, 0).reshape(b, BRANCH_W, t)


def _rec_order(a, axis=-1):
    a = jnp.moveaxis(a, axis, -1)
    lead = a.shape[:-1]
    a = jnp.swapaxes(a.reshape(lead + (N_HEADS, 2, HALF_J)), -1, -3).reshape(lead + (BRANCH_W,))
    return jnp.moveaxis(a, -1, axis)


def _sb_body(tq, tk, q_ref, k_ref, v_ref, qg_ref, kg_ref, u_ref, o_ref,
             kn_scr, vb_scr, qn_scr, hl_scr, lsig_scr, logit_scr, run_scr, acc_scr):
    qi = pl.program_id(2)
    nsub = tq // tk
    nblk = kn_scr.shape[0]
    lane = lax.broadcasted_iota(jnp.int32, (1, LANES), 1)
    first = lane < HEAD_DIM
    m_a = first.astype(F32)
    m_b = 1.0 - m_a

    def headnorm(x, g):
        x2 = x * x
        s_a = jnp.sum(x2 * m_a, axis=-1, keepdims=True)
        s_b = jnp.sum(x2 * m_b, axis=-1, keepdims=True)
        ms = jnp.where(first, s_a, s_b) * (1.0 / HEAD_DIM)
        return x * lax.rsqrt(ms + NORM_EPS) * g

    @pl.when(qi == 0)
    def _():
        kn = headnorm(k_ref[0].astype(F32), kg_ref[...])
        v = v_ref[0].astype(F32)
        kn_scr[:, :tk, :] = (kn * m_a).astype(BF16).reshape(nblk, tk, LANES)
        kn_scr[:, tk:, :] = (kn * m_b).astype(BF16).reshape(nblk, tk, LANES)
        vb_scr[:, :tk, :] = (v * m_a).astype(BF16).reshape(nblk, tk, LANES)
        vb_scr[:, tk:, :] = (v * m_b).astype(BF16).reshape(nblk, tk, LANES)

    qn_scr[...] = (headnorm(q_ref[0].astype(F32), qg_ref[...]) * (LOG2E * HEAD_DIM ** -0.5)).astype(BF16)
    run_scr[...] = jnp.zeros_like(run_scr)
    acc_scr[...] = jnp.zeros_like(acc_scr)
    uu = u_ref[...]
    rowi = lax.broadcasted_iota(jnp.int32, (SB_ROWS, 2 * tk), 0)
    coli = lax.broadcasted_iota(jnp.int32, (SB_ROWS, 2 * tk), 1) & (tk - 1)

    last = qi * nsub + (nsub - 1)
    heads = (slice(0, tk), slice(tk, 2 * tk))
    groups = [slice(s * SB_ROWS, (s + 1) * SB_ROWS) for s in range(tq // SB_ROWS)]

    def score(m, masked, rows):
        z = lax.dot_general(qn_scr[rows, :], kn_scr[jnp.maximum(last - m, 0)], NT_DIMS,
                            preferred_element_type=F32)
        l1m = -(jnp.maximum(z, 0.0) + jnp.log2(1.0 + jnp.exp2(-jnp.abs(z))))
        lsig = z + l1m
        if masked:
            mask = (coli + ((nsub - 1 - m) * tk - rows.start)) < rowi
            l1m = jnp.where(mask, l1m, 0.0)
            lsig = jnp.where(mask, lsig, MASKED_LOGIT)
        hi = l1m.astype(BF16)
        lo = (l1m - hi.astype(F32)).astype(BF16)
        for h, sl in enumerate(heads):
            hl_scr[h, rows, :] = jnp.concatenate([hi[:, sl], lo[:, sl]], axis=1)
        lsig_scr[rows, :] = lsig

    def suffix(rows):
        for h, sl in enumerate(heads):
            ce = _dot(hl_scr[h, rows, :], uu)
            run = run_scr[h, rows, :]
            logit_scr[rows, sl] = lsig_scr[rows, sl] + ce[:, :tk] + run
            run_scr[h, rows, :] = run + ce[:, tk:]

    def weigh(m, rows):
        acc_scr[rows, :] += _dot(jnp.exp2(logit_scr[rows, :]).astype(BF16), vb_scr[last - m])

    def body(m, carry):
        for rows in groups:
            weigh(m, rows)
            suffix(rows)
            score(m + 2, False, rows)
        return carry

    def live(rows, m):
        return (nsub - 1 - m) * tk <= rows.stop - 2

    def cut(rows, m):
        return (nsub - m) * tk - 1 >= rows.start

    for rows in groups:
        if live(rows, 0):
            score(0, cut(rows, 0), rows)
    for rows in groups:
        if live(rows, 0):
            suffix(rows)
        if live(rows, 1):
            score(1, cut(rows, 1), rows)
    for m in range(nsub):
        for rows in groups:
            if live(rows, m):
                weigh(m, rows)
            if live(rows, m + 1):
                suffix(rows)
            if live(rows, m + 2):
                score(m + 2, cut(rows, m + 2), rows)
    lax.fori_loop(nsub, (qi + 1) * nsub, body, 0)
    o_ref[0] = acc_scr[...]


def _sb_attention(P, qg, kg, uu, tq=1024, tk=LANES):
    b, t, _ = P.shape
    tq = min(tq, t)
    npair = N_HEADS // 2
    qoff, koff, voff = 0, BRANCH_W // LANES, 2 * BRANCH_W // LANES
    stacked = pltpu.VMEM((t // tk, 2 * tk, LANES), BF16)
    return pl.pallas_call(
        functools.partial(_sb_body, tq, tk),
        grid=(b, npair, t // tq),
        in_specs=[
            pl.BlockSpec((1, tq, LANES), lambda i, h, j: (i, j, qoff + h)),
            pl.BlockSpec((1, t, LANES), lambda i, h, j: (i, 0, koff + h)),
            pl.BlockSpec((1, t, LANES), lambda i, h, j: (i, 0, voff + h)),
            pl.BlockSpec((1, LANES), lambda i, h, j: (0, 0)),
            pl.BlockSpec((1, LANES), lambda i, h, j: (0, 0)),
            pl.BlockSpec((2 * tk, 2 * tk), lambda i, h, j: (0, 0)),
        ],
        out_specs=pl.BlockSpec((1, tq, LANES), lambda i, h, j: (i, j, h)),
        out_shape=jax.ShapeDtypeStruct((b, t, BRANCH_W), F32),
        scratch_shapes=[stacked, stacked,
                        pltpu.VMEM((tq, LANES), BF16),
                        pltpu.VMEM((2, tq, 2 * tk), BF16),
                        pltpu.VMEM((tq, 2 * tk), F32),
                        pltpu.VMEM((tq, 2 * tk), F32),
                        pltpu.VMEM((2, tq, LANES), F32),
                        pltpu.VMEM((tq, LANES), F32)],
        compiler_params=_params("parallel", "parallel", "arbitrary"),
        cost_estimate=pl.CostEstimate(flops=b * N_HEADS * t * t * (2 * HEAD_DIM + 4 * tk + 2 * HEAD_DIM) // 2,
                                      transcendentals=3 * b * N_HEADS * t * t // 2,
                                      bytes_accessed=2 * 3 * b * t * BRANCH_W + 4 * b * t * BRANCH_W),
        name="sb_attn",
    )(P, P, P, qg, kg, uu)


def _memkv_body(m_ref, g_ref, w_ref, kg_ref, k_o, v_o):
    kv = _dot(_rms(m_ref[0], g_ref[...]).astype(BF16), w_ref[...])
    for h in range(MEM_HEADS):
        sl = slice(h * MEM_HD, (h + 1) * MEM_HD)
        k_o[0, :, sl] = _rms(kv[:, sl], kg_ref[...]).astype(BF16)
    v_o[0] = kv[:, BRANCH_W:].astype(BF16)


def _mem_kv(mem, g, w_bf16, kg):
    b = mem.shape[0]
    out = jax.ShapeDtypeStruct((b, MEM_LEN, BRANCH_W), BF16)
    blk = pl.BlockSpec((1, MEM_LEN, BRANCH_W), lambda i: (i, 0, 0))
    return pl.pallas_call(
        _memkv_body,
        grid=(b,),
        in_specs=[
            pl.BlockSpec((1, MEM_LEN, D_MODEL), lambda i: (i, 0, 0)),
            pl.BlockSpec((1, D_MODEL), lambda i: (0, 0)),
            pl.BlockSpec((D_MODEL, 2 * BRANCH_W), lambda i: (0, 0)),
            pl.BlockSpec((1, MEM_HD), lambda i: (0, 0)),
        ],
        out_specs=[blk, blk],
        out_shape=[out, out],
        compiler_params=_params("parallel"),
        name="mem_kv",
    )(mem, g, w_bf16, kg)


def _memattn_body(q_ref, k_ref, v_ref, qg_ref, o_ref):
    q = q_ref[0].astype(F32)
    for h in range(MEM_HEADS):
        sl = slice(h * MEM_HD, (h + 1) * MEM_HD)
        qh = (_rms(q[:, sl], qg_ref[...]) * (MEM_HD ** -0.5)).astype(BF16)
        s = lax.dot_general(qh, k_ref[0, :, sl], NT_DIMS, preferred_element_type=F32)
        p = jnp.exp(s - jnp.max(s, axis=-1, keepdims=True))
        den = jnp.sum(p, axis=-1, keepdims=True)
        o_ref[0, :, sl] = _dot(p.astype(BF16), v_ref[0, :, sl]) / den


def _mem_attention(P, kn, mv, qg, tq=1024):
    b, t, _ = P.shape
    qoff = MEM_Q_OFF // BRANCH_W
    kv = pl.BlockSpec((1, MEM_LEN, BRANCH_W), lambda i, j: (i, 0, 0))
    return pl.pallas_call(
        _memattn_body,
        grid=(b, t // tq),
        in_specs=[
            pl.BlockSpec((1, tq, BRANCH_W), lambda i, j: (i, j, qoff)),
            kv, kv,
            pl.BlockSpec((1, MEM_HD), lambda i, j: (0, 0)),
        ],
        out_specs=pl.BlockSpec((1, tq, BRANCH_W), lambda i, j: (i, j, 0)),
        out_shape=jax.ShapeDtypeStruct((b, t, BRANCH_W), F32),
        compiler_params=_params("parallel", "parallel"),
        name="mem_attn",
    )(P, kn, mv, qg)


def _merge_body(x_ref, y_ref, bonus_ref, g_ref, yb_ref, ym_ref, g0_ref, g1_ref, g2_ref,
                lng_ref, lnb_ref, bd_ref, wbr_ref, wout_ref, o_ref):
    bd = bd_ref[...]
    y = y_ref[0].T
    d = y - _dot_ones2(y, bd) * (1.0 / HEAD_DIM)
    var = _dot_ones2(d * d, bd) * (1.0 / HEAD_DIM)
    ya = (d * lax.rsqrt(var + LNX_EPS) * lng_ref[...] + lnb_ref[...] + bonus_ref[0].astype(F32)) * g_ref[0].astype(F32)
    m = (_sigmoid(g0_ref[0].astype(F32)) * _dot(ya.astype(BF16), wbr_ref[0])
         + _sigmoid(g1_ref[0].astype(F32)) * _dot(yb_ref[0].astype(BF16), wbr_ref[1])
         + _sigmoid(g2_ref[0].astype(F32)) * _dot(ym_ref[0].astype(BF16), wbr_ref[2]))
    o_ref[0] = x_ref[0] + _dot(m.astype(BF16), wout_ref[...])


def _merge(x, y, bonus, g, yb, ym, gates, lng, lnb, bd, wbr, wout, tq=1024):
    b, t, _ = x.shape
    tok = pl.BlockSpec((1, tq, BRANCH_W), lambda i, j: (i, j, 0))
    xblk = pl.BlockSpec((1, tq, D_MODEL), lambda i, j: (i, j, 0))
    goff = GATE_OFF // D_MODEL

    def gate(n):
        return pl.BlockSpec((1, tq, D_MODEL), lambda i, j: (i, j, goff + n))

    def full(a):
        return pl.BlockSpec(a.shape, lambda i, j: (0,) * a.ndim)

    return pl.pallas_call(
        _merge_body,
        grid=(b, t // tq),
        in_specs=[xblk, pl.BlockSpec((1, BRANCH_W, tq), lambda i, j: (i, 0, j)), tok, tok, tok, tok,
                  gate(0), gate(1), gate(2),
                  full(lng), full(lnb), full(bd), full(wbr), full(wout)],
        out_specs=xblk,
        out_shape=jax.ShapeDtypeStruct(x.shape, F32),
        compiler_params=_params("parallel", "parallel"),
        name="merge",
    )(x, y, bonus, g, yb, ym, gates, gates, gates, lng, lnb, bd, wbr, wout)


FF_TILE = 2816


def _ffn_body(tq, x_ref, g_ref, wup_ref, cw_ref, cb_ref, wdn_ref, o_ref, carry):
    @pl.when(pl.program_id(1) == 0)
    def _():
        carry[...] = jnp.zeros_like(carry)

    x = x_ref[0]
    h = _rms(x, g_ref[...]).astype(BF16)
    row = lax.broadcasted_iota(jnp.int32, (tq, 1), 0)
    acc = jnp.zeros((tq, D_MODEL), F32)
    for f in range(D_FF // FF_TILE):
        halves = []
        for half in range(2):
            c0 = half * D_FF + f * FF_TILE
            sl = slice(c0, c0 + FF_TILE)
            hh = _dot(h, wup_ref[:, sl])
            cp = carry[:, sl]
            s1 = jnp.where(row == 0, cp[7:8], pltpu.roll(hh, 1, 0))
            s2 = jnp.where(row == 0, cp[6:7], jnp.where(row == 1, cp[7:8], pltpu.roll(hh, 2, 0)))
            carry[:, sl] = hh[tq - 8:, :]
            cw = cw_ref[:, sl]
            halves.append(cb_ref[:, sl] + s2 * cw[0:1] + s1 * cw[1:2] + hh * cw[2:3])
        gate, val = halves
        u = gate * _sigmoid(gate) * val
        acc = acc + _dot(u.astype(BF16), wdn_ref[f * FF_TILE:(f + 1) * FF_TILE, :])
    o_ref[0] = x + acc


def _ffn(x, g, wup, cw, cb, wdn, tq=512):
    b, t, _ = x.shape
    xblk = pl.BlockSpec((1, tq, D_MODEL), lambda i, j: (i, j, 0))

    def full(a):
        return pl.BlockSpec(a.shape, lambda i, j: (0,) * a.ndim, pipeline_mode=pl.Buffered(1))

    return pl.pallas_call(
        functools.partial(_ffn_body, tq),
        grid=(b, t // tq),
        in_specs=[xblk, full(g), full(wup), full(cw), full(cb), full(wdn)],
        out_specs=xblk,
        out_shape=jax.ShapeDtypeStruct(x.shape, F32),
        scratch_shapes=[pltpu.VMEM((8, 2 * D_FF), F32)],
        compiler_params=_params("parallel", "arbitrary"),
        name="conv_ffn",
    )(x, g, wup, cw, cb, wdn)


def _row(v):
    return v.reshape(1, -1)


def kernel(x, mem, norm1_g, w_in, shift_mu, decay_w0, decay_w2, iclr_a0, iclr_a2, gate_g2, k_k, k_a, r_k, lnx_g, lnx_b, vres_v0, vres_v1, vres_v2, sb_q_norm_g, sb_k_norm_g, mem_norm_g, w_mem_kv, mem_q_norm_g, mem_k_norm_g, w_branch, w_out, norm2_g, w_up, conv_w, conv_b, w_down):
    b, t, _ = x.shape
    assert 2 * b * N_HEADS == LANES, "recurrence layout packs (key-half, batch, head) onto the lanes"
    depth = w_in.shape[0]

    head_id = jnp.arange(BRANCH_W) % N_HEADS
    bd = (head_id[:, None] == head_id[None, :]).astype(BF16)
    tri = jnp.arange(LANES)
    u_ext = jnp.concatenate([(tri[:, None] > tri[None, :]).astype(BF16),
                             jnp.ones((LANES, LANES), BF16)], axis=1)
    u_ext = jnp.concatenate([u_ext, u_ext], axis=0)
    zpad64 = jnp.zeros((64, BRANCH_W), F32)

    def rkv_order(a):
        head = _rec_order(a[..., :3 * BRANCH_W].reshape(a.shape[:-1] + (3, BRANCH_W)))
        return jnp.concatenate([head.reshape(a.shape[:-1] + (3 * BRANCH_W,)), a[..., 3 * BRANCH_W:]], axis=-1)

    w_rwkv_all = jnp.pad(rkv_order(w_in[..., :RWKV_COLS]), ((0, 0), (0, 0), (0, RWKV_PAD - RWKV_COLS)))
    w_branch_r = w_branch.at[:, 0].set(_rec_order(w_branch[:, 0], axis=1)).reshape(depth, -1, D_MODEL)

    v_first = None
    for l in range(depth):
        x2d = x.reshape(b * t, D_MODEL)
        P, Pr = _proj(x2d, _row(norm1_g[l]), _to_bf16(w_rwkv_all, l), _to_bf16(w_in, l, RWKV_COLS))
        P, Pr = P.reshape(b, t, RWKV_PAD), Pr.reshape(b, t, REST_COLS)

        vecs = (_row(rkv_order(shift_mu[l])), _row(_rec_order(decay_w0[l])), _row(_rec_order(iclr_a0[l])),
                _row(_rec_order(k_k[l])), _row(_rec_order(k_a[l])), _row(_rec_order(r_k[l].reshape(-1))))
        mats = (jnp.concatenate([_rec_order(decay_w2[l]), zpad64], axis=0).astype(BF16),
                jnp.concatenate([zpad64, _rec_order(iclr_a2[l])], axis=0).astype(BF16),
                _rec_order(gate_g2[l]).astype(BF16))
        vres = None
        if l > 0:
            v1 = jnp.pad(_rec_order(vres_v1[l - 1], axis=0), ((0, 0), (0, LANES - vres_v1.shape[-1]))).astype(BF16)
            v2 = jnp.pad(_rec_order(vres_v2[l - 1]), ((0, LANES - vres_v2.shape[-2]), (0, 0))).astype(BF16)
            vres = (v_first, _row(_rec_order(vres_v0[l - 1])), v1, v2)
        r, dec, k2, v, nkk, bb, g, bonus, *v_plain = _prep(P, vecs, mats, bd, vres)
        if l == 0:
            v_first = v_plain[0]

        kn, mv = _mem_kv(mem, _row(mem_norm_g[l]), _to_bf16(w_mem_kv, l), _row(mem_k_norm_g[l]))
        ym = _mem_attention(Pr, kn, mv, _row(mem_q_norm_g[l]))

        lanes = lambda a: a.reshape(HALF_J, LANES, t)
        y = _recurrence(lanes(dec), lanes(nkk), lanes(bb), lanes(k2), lanes(r), _value_lanes(v), ym)
        y = _from_value_lanes(y, b)

        qg2 = _row(jnp.concatenate([sb_q_norm_g[l], sb_q_norm_g[l]]))
        kg2 = _row(jnp.concatenate([sb_k_norm_g[l], sb_k_norm_g[l]]))
        yb = _sb_attention(Pr, qg2, kg2, u_ext)

        wbr = _to_bf16(w_branch_r, l).reshape(w_branch.shape[1:])
        x = _merge(x, y, bonus, g, yb, ym, Pr, _row(_rec_order(lnx_g[l])), _row(_rec_order(lnx_b[l])), bd, wbr,
                   _to_bf16(w_out, l))
        x = _ffn(x, _row(norm2_g[l]), _to_bf16(w_up, l), conv_w[l], _row(conv_b[l]), _to_bf16(w_down, l))
    return x
```
